```python
import jax, jax.numpy as jnp
from jax import lax
import numpy as np

D_MODEL = 1024
BATCH = 8
SEQ = 8192
DEPTH = 1

N_MEM = 256
D_MIX = D_MODEL
ML_HEADS = 4
ML_DH = D_MIX // 2 // ML_HEADS
ML_W = ML_HEADS * ML_DH
ML_CHUNK = 64
CONV_W = 4
FX_HEADS = 8
FX_DH = (D_MIX - ML_W) // FX_HEADS
FX_W = FX_HEADS * FX_DH
Q_BLOCK = 128
X_HEADS = 4
X_DH = D_MODEL // X_HEADS
D_FF = 4 * D_MODEL
EPS = 1e-6
IN_SIZES = (2 * ML_W, ML_W, ML_W, ML_HEADS, ML_HEADS, FX_W, FX_W, FX_W, FX_HEADS)
IN_COLS = 4 * ML_W + 2 * ML_HEADS + 3 * FX_W + FX_HEADS

kernel_name = "hymba_mlstm_fox_hybrid"


def rmsnorm(x, g):
    xf = x.astype(jnp.float32)
    y = xf * lax.rsqrt(jnp.mean(xf * xf, axis=-1, keepdims=True) + EPS)
    return (y * g.astype(jnp.float32)).astype(x.dtype)


def causal_conv(u, w, b):
    S = u.shape[1]
    up = jnp.pad(u, ((0, 0), (CONV_W - 1, 0), (0, 0)))
    out = b
    for j in range(CONV_W):
        out = out + up[:, j:j + S] * w[j]
    return out


def mlstm_chunkwise(q, k, v, ig, lf):
    B, S, H, dh = q.shape
    nc = S // ML_CHUNK
    k = k * (dh ** -0.5)

    def to_chunks(t):
        t = t.reshape((B, nc, ML_CHUNK) + t.shape[2:])
        return jnp.moveaxis(t, (1, 3), (0, 2))

    causal = jnp.tril(jnp.ones((ML_CHUNK, ML_CHUNK), dtype=bool))

    def body(carry, xs):
        C, n, m = carry
        qc, kc, vc, ic, fc = xs
        b = jnp.cumsum(fc, axis=-1)
        g = b[..., -1]
        dmat = b[..., :, None] - b[..., None, :] + ic[..., None, :]
        dmat = jnp.where(causal, dmat, -jnp.inf)
        m_inter = b + m[..., None]
        m_t = jnp.maximum(m_inter, jnp.max(dmat, axis=-1))
        scores = jnp.einsum('bhtd,bhsd->bhts', qc, kc) * jnp.exp(dmat - m_t[..., None])
        inter = jnp.exp(m_inter - m_t)
        num = (jnp.einsum('bhts,bhsd->bhtd', scores, vc)
               + inter[..., None] * jnp.einsum('bhed,bhtd->bhte', C, qc))
        den = scores.sum(-1) + inter * jnp.einsum('bhd,bhtd->bht', n, qc)
        h = num / jnp.maximum(jnp.abs(den), jnp.exp(-m_t))[..., None]
        a = g[..., None] - b + ic
        m_new = jnp.maximum(g + m, jnp.max(a, axis=-1))
        decay = jnp.exp(g + m - m_new)
        wa = jnp.exp(a - m_new[..., None])
        C_new = decay[..., None, None] * C + jnp.einsum('bhs,bhse,bhsd->bhed', wa, vc, kc)
        n_new = decay[..., None] * n + jnp.einsum('bhs,bhsd->bhd', wa, kc)
        return (C_new, n_new, m_new), h

    init = (jnp.zeros((B, H, dh, dh), jnp.float32), jnp.zeros((B, H, dh), jnp.float32),
            jnp.zeros((B, H), jnp.float32))
    _, hs = lax.scan(body, init, (to_chunks(q), to_chunks(k), to_chunks(v), to_chunks(ig), to_chunks(lf)))
    hs = jnp.moveaxis(hs, (0, 2), (1, 3))
    return hs.reshape(B, S, H, dh)


def forgetting_attention(q, k, v, lf):
    B, S, H, dh = q.shape
    nb = S // Q_BLOCK
    cT = jnp.cumsum(lf, axis=1).transpose(0, 2, 1)
    qb = q.reshape(B, nb, Q_BLOCK, H, dh).transpose(1, 0, 2, 3, 4)
    cb = cT.reshape(B, H, nb, Q_BLOCK).transpose(2, 0, 1, 3)
    kpos = jnp.arange(S)
    scale = dh ** -0.5

    def block(args):
        qi, ci, blk = args
        qpos = blk * Q_BLOCK + jnp.arange(Q_BLOCK)
        s = jnp.einsum('bqhd,bkhd->bhqk', qi, k, preferred_element_type=jnp.float32) * scale
        s = s + (ci[..., :, None] - cT[..., None, :])
        s = jnp.where(qpos[:, None] >= kpos[None, :], s, -jnp.inf)
        p = jax.nn.softmax(s, axis=-1)
        return jnp.einsum('bhqk,bkhd->bqhd', p.astype(v.dtype), v)

    out = lax.map(block, (qb, cb, jnp.arange(nb)))
    return out.transpose(1, 0, 2, 3, 4).reshape(B, S, H, dh)


def cross_attention(h, memn, w_q, w_kv, w_o):
    B, S, D = h.shape
    M = memn.shape[1]
    q = (h @ w_q).reshape(B, S, X_HEADS, X_DH)
    k, v = jnp.split(memn @ w_kv, 2, axis=-1)
    k = k.reshape(B, M, X_HEADS, X_DH)
    v = v.reshape(B, M, X_HEADS, X_DH)
    s = jnp.einsum('bqhd,bmhd->bhqm', q, k, preferred_element_type=jnp.float32) * (X_DH ** -0.5)
    p = jax.nn.softmax(s, axis=-1)
    o = jnp.einsum('bhqm,bmhd->bqhd', p.astype(v.dtype), v).reshape(B, S, D)
    return o @ w_o


def setup_inputs(seed: int = 0) -> dict:
    key = jax.random.key(seed)
    ks = jax.random.split(key, 24)
    f32 = jnp.float32

    def w(k, shape, fan_in):
        return jax.random.normal(k, shape, f32) * (fan_in ** -0.5)

    def gain(k, shape):
        return 1.0 + 0.1 * jax.random.normal(k, shape, f32)

    def small(k, shape):
        return 0.01 * jax.random.normal(k, shape, f32)

    return {
        "x": jax.random.normal(ks[0], (BATCH, SEQ, D_MODEL), f32),
        "mem": jax.random.normal(ks[1], (BATCH, N_MEM, D_MODEL), f32),
        "ln1": gain(ks[2], (DEPTH, D_MODEL)),
        "w_in": w(ks[3], (DEPTH, D_MODEL, IN_COLS), D_MODEL),
        "ml_conv_w": w(ks[4], (DEPTH, CONV_W, 2 * ML_W), CONV_W),
        "ml_conv_b": small(ks[5], (DEPTH, 2 * ML_W)),
        "ml_b_i": small(ks[6], (DEPTH, ML_HEADS)),
        "ml_b_f": 3.0 + 0.1 * jax.random.normal(ks[7], (DEPTH, ML_HEADS), f32),
        "ml_norm": gain(ks[8], (DEPTH, ML_W)),
        "fx_b_f": 3.0 + 0.1 * jax.random.normal(ks[9], (DEPTH, FX_HEADS), f32),
        "w_out": w(ks[10], (DEPTH, D_MIX, D_MODEL), D_MIX),
        "ln_x": gain(ks[11], (DEPTH, D_MODEL)),
        "ln_mem": gain(ks[12], (DEPTH, D_MODEL)),
        "w_xq": w(ks[13], (DEPTH, D_MODEL, D_MODEL), D_MODEL),
        "w_xkv": w(ks[14], (DEPTH, D_MODEL, 2 * D_MODEL), D_MODEL),
        "w_xo": w(ks[15], (DEPTH, D_MODEL, D_MODEL), D_MODEL),
        "ln2": gain(ks[16], (DEPTH, D_MODEL)),
        "w_ff1": w(ks[17], (DEPTH, D_MODEL, D_FF), D_MODEL),
        "w_ff2": w(ks[18], (DEPTH, D_FF, D_MODEL), D_FF),
        "ln_f": gain(ks[19], (D_MODEL,)),
    }


def reference(x, mem, ln1, w_in, ml_conv_w, ml_conv_b, ml_b_i, ml_b_f, ml_norm, fx_b_f, w_out,
              ln_x, ln_mem, w_xq, w_xkv, w_xo, ln2, w_ff1, w_ff2, ln_f):
    B, S, _ = x.shape
    splits = [int(v) for v in np.cumsum(IN_SIZES)[:-1]]
    for l in range(DEPTH):
        h = rmsnorm(x, ln1[l])
        z = h @ w_in[l]
        ml_qk, ml_v, ml_o, ml_i, ml_f, fx_q, fx_k, fx_v, fx_f = jnp.split(z, splits, axis=-1)

        ml_qk = jax.nn.silu(causal_conv(ml_qk, ml_conv_w[l], ml_conv_b[l]))
        mq, mk = jnp.split(ml_qk.astype(jnp.float32), 2, axis=-1)
        mq = mq.reshape(B, S, ML_HEADS, ML_DH)
        mk = mk.reshape(B, S, ML_HEADS, ML_DH)
        mv = ml_v.astype(jnp.float32).reshape(B, S, ML_HEADS, ML_DH)
        ig = ml_i.astype(jnp.float32) + ml_b_i[l].astype(jnp.float32)
        lf = jax.nn.log_sigmoid(ml_f.astype(jnp.float32) + ml_b_f[l].astype(jnp.float32))
        mh = mlstm_chunkwise(mq, mk, mv, ig, lf)
        mh = mh * lax.rsqrt(jnp.mean(mh * mh, axis=-1, keepdims=True) + EPS)
        mh = mh * ml_norm[l].astype(jnp.float32).reshape(ML_HEADS, ML_DH)
        ml_out = (mh.reshape(B, S, ML_W) * jax.nn.sigmoid(ml_o.astype(jnp.float32))).astype(x.dtype)

        fq = fx_q.reshape(B, S, FX_HEADS, FX_DH)
        fk = fx_k.reshape(B, S, FX_HEADS, FX_DH)
        fv = fx_v.reshape(B, S, FX_HEADS, FX_DH)
        flf = jax.nn.log_sigmoid(fx_f.astype(jnp.float32) + fx_b_f[l].astype(jnp.float32))
        fx_out = forgetting_attention(fq, fk, fv, flf).reshape(B, S, FX_W)

        x = x + jnp.concatenate([ml_out, fx_out], axis=-1) @ w_out[l]

        x = x + cross_attention(rmsnorm(x, ln_x[l]), rmsnorm(mem, ln_mem[l]), w_xq[l], w_xkv[l], w_xo[l])

        u = jax.nn.relu(rmsnorm(x, ln2[l]) @ w_ff1[l])
        x = x + (u * u) @ w_ff2[l]
    return rmsnorm(x, ln_f)
```

```python
import functools

import jax
import jax.numpy as jnp
from jax import lax
from jax.experimental import pallas as pl
from jax.experimental.pallas import tpu as pltpu

F32 = jnp.float32
BF16 = jnp.bfloat16

D_MODEL = 1024
ML_HEADS = 4
ML_DH = 128
ML_W = ML_HEADS * ML_DH
CONV_W = 4
FX_HEADS = 8
FX_DH = 64
FX_W = FX_HEADS * FX_DH
X_HEADS = 4
X_DH = D_MODEL // X_HEADS
D_FF = 4 * D_MODEL
EPS = 1e-6

LANES = 128
NEG_BIG = -1e30
VMEM_LIMIT = 56 * 1024 * 1024

_C_QK = 0
_C_V = _C_QK + 2 * ML_W
_C_O = _C_V + ML_W
_C_FQ = _C_O + ML_W
_C_FK = _C_FQ + FX_W
_C_G = _C_FK + FX_W
_C_END = _C_G + LANES
_R_GT = FX_W
_N_GT = 16

_AUG0 = FX_DH


def _tiles(S):
    return dict(
        tm=min(512, S),
        chunk=min(256, S),
        tkv=min(256, S),
        tail=min(512, S),
    )


def _rms(x, g):
    return x * lax.rsqrt(jnp.mean(x * x, axis=-1, keepdims=True) + EPS) * g


def _split3(x):
    hi = x.astype(BF16)
    r = x - hi.astype(F32)
    mid = r.astype(BF16)
    lo = (r - mid.astype(F32)).astype(BF16)
    return hi, mid, lo


def _log_sigmoid(x):
    return jnp.minimum(x, 0.0) - jnp.log1p(jnp.exp(-jnp.abs(x)))


def _const_spec(shape):
    nd = len(shape)
    return pl.BlockSpec(shape, lambda *_: (0,) * nd, pipeline_mode=pl.Buffered(1))


def _proj_kernel(x_ref, g_ref, w_ref, wt_ref, brow_ref, bcol_ref,
                 qk_ref, v_ref, o_ref, gate_ref, gatet_ref, fq_ref, fk_ref, fvt_ref,
                 carry_ref, *, tm, tkv):
    @pl.when(pl.program_id(1) == 0)
    def _():
        carry_ref[...] = jnp.zeros_like(carry_ref)

    x = x_ref[0]
    h = _rms(x, g_ref[...]).astype(BF16)

    def proj(c0, c1):
        return jnp.dot(h, w_ref[:, c0:c1], preferred_element_type=F32)

    qk_ref[0] = proj(_C_QK, _C_V).astype(BF16)
    v_ref[0] = proj(_C_V, _C_O).astype(BF16)
    o_ref[0] = proj(_C_O, _C_FQ).astype(BF16)

    lane = lax.broadcasted_iota(jnp.int32, (tm, LANES), 1)
    gpre = proj(_C_G, _C_END) + brow_ref[...]
    gls = _log_sigmoid(gpre)
    gate_ref[0] = jnp.where(lane < ML_HEADS, gpre, gls)

    zt = lax.dot_general(wt_ref[...], h, (((1,), (1,)), ((), ())), preferred_element_type=F32)
    for j in range(tm // tkv):
        fvt_ref[0, j] = zt[0:FX_W, j * tkv:(j + 1) * tkv].astype(BF16)
    gt = zt[_R_GT:_R_GT + _N_GT, :] + bcol_ref[...]
    row = lax.broadcasted_iota(jnp.int32, (_N_GT, tm), 0)
    gatet_ref[0] = jnp.where(row < ML_HEADS, gt, _log_sigmoid(gt))

    r_i = lax.broadcasted_iota(jnp.int32, (tm, tm), 0)
    c_i = lax.broadcasted_iota(jnp.int32, (tm, tm), 1)
    tril = (c_i <= r_i).astype(BF16)
    csum = carry_ref[...] + sum(
        jnp.dot(tril, part, preferred_element_type=F32) for part in _split3(gls))
    carry_ref[...] = csum[tm - 1:tm, :]
    c_hi, c_mid, c_lo = (p.astype(F32) for p in _split3(csum))

    zq = proj(_C_FQ, _C_FK) * (FX_DH ** -0.5)
    zk = proj(_C_FK, _C_G)
    ones3 = ((lane >= _AUG0 + 3) & (lane < _AUG0 + 6)).astype(F32)
    ones3k = ((lane >= _AUG0) & (lane < _AUG0 + 3)).astype(F32)
    for hd in range(FX_HEADS):
        grp = hd // 2
        zq_h = zq[:, grp * LANES:(grp + 1) * LANES]
        zk_h = zk[:, grp * LANES:(grp + 1) * LANES]
        if hd % 2:
            zq_h = pltpu.roll(zq_h, LANES // 2, 1)
            zk_h = pltpu.roll(zk_h, LANES // 2, 1)
        gl = 2 * ML_HEADS + hd
        hi_c, mid_c, lo_c = (c[:, gl:gl + 1] for c in (c_hi, c_mid, c_lo))
        qa = jnp.where(lane < _AUG0, zq_h,
             jnp.where(lane == _AUG0, hi_c,
             jnp.where(lane == _AUG0 + 1, mid_c,
             jnp.where(lane == _AUG0 + 2, lo_c, ones3))))
        ka = jnp.where(lane < _AUG0, zk_h,
             jnp.where(lane == _AUG0 + 3, -hi_c,
             jnp.where(lane == _AUG0 + 4, -mid_c,
             jnp.where(lane == _AUG0 + 5, -lo_c, ones3k))))
        fq_ref[0, hd] = qa.astype(BF16)
        fk_ref[0, hd] = ka.astype(BF16)


def _proj(x, ln1, w_main, w_t, brow, bcol, *, tm, tkv):
    B, S, _ = x.shape
    ns = S // tm
    r = tm // tkv
    kern = functools.partial(_proj_kernel, tm=tm, tkv=tkv)
    row_spec = lambda n: pl.BlockSpec((1, tm, n), lambda b, s: (b, s, 0))
    out_shape = (
        jax.ShapeDtypeStruct((B, S, 2 * ML_W), BF16),
        jax.ShapeDtypeStruct((B, S, ML_W), BF16),
        jax.ShapeDtypeStruct((B, S, ML_W), BF16),
        jax.ShapeDtypeStruct((B, S, LANES), F32),
        jax.ShapeDtypeStruct((B, _N_GT, S), F32),
        jax.ShapeDtypeStruct((B, FX_HEADS, S, LANES), BF16),
        jax.ShapeDtypeStruct((B, FX_HEADS, S, LANES), BF16),
        jax.ShapeDtypeStruct((B, S // tkv, FX_W, tkv), BF16),
    )
    out_specs = (
        row_spec(2 * ML_W), row_spec(ML_W), row_spec(ML_W), row_spec(LANES),
        pl.BlockSpec((1, _N_GT, tm), lambda b, s: (b, 0, s)),
        pl.BlockSpec((1, FX_HEADS, tm, LANES), lambda b, s: (b, 0, s, 0)),
        pl.BlockSpec((1, FX_HEADS, tm, LANES), lambda b, s: (b, 0, s, 0)),
        pl.BlockSpec((1, r, FX_W, tkv), lambda b, s: (b, s, 0, 0)),
    )
    return pl.pallas_call(
        kern,
        grid=(B, ns),
        in_specs=[
            row_spec(D_MODEL),
            _const_spec((1, D_MODEL)),
            _const_spec(w_main.shape),
            _const_spec(w_t.shape),
            _const_spec((1, LANES)),
            _const_spec((_N_GT, 1)),
        ],
        out_specs=out_specs,
        out_shape=out_shape,
        scratch_shapes=[pltpu.VMEM((1, LANES), F32)],
        compiler_params=pltpu.CompilerParams(
            dimension_semantics=("arbitrary", "arbitrary"), vmem_limit_bytes=VMEM_LIMIT),
        name="proj",
    )(x, ln1, w_main, w_t, brow, bcol)


_TAIL = 8


def _mlstm_kernel(qk_ref, v_ref, o_ref, gate_ref, gatet_ref, cw_ref, cb_ref, nrm_ref,
                  out_ref, xbuf_ref, state_ref, m_ref, *, L):
    @pl.when(pl.program_id(1) == 0)
    def _():
        xbuf_ref[0:_TAIL, :] = jnp.zeros((_TAIL, 2 * ML_W), F32)
        state_ref[...] = jnp.zeros_like(state_ref)
        m_ref[...] = jnp.zeros_like(m_ref)

    xbuf_ref[_TAIL:_TAIL + L, :] = qk_ref[0].astype(F32)
    conv = cb_ref[...]
    for j in range(CONV_W):
        off = _TAIL - (CONV_W - 1) + j
        conv = conv + xbuf_ref[off:off + L, :] * cw_ref[j:j + 1, :]
    xbuf_ref[0:_TAIL, :] = xbuf_ref[L:L + _TAIL, :]
    qk = conv * jax.nn.sigmoid(conv)

    gate = gate_ref[0]
    gatet = gatet_ref[0]
    r_i = lax.broadcasted_iota(jnp.int32, (L, L), 0)
    c_i = lax.broadcasted_iota(jnp.int32, (L, L), 1)
    causal = c_i <= r_i
    tril = causal.astype(BF16)
    triu = (r_i <= c_i).astype(BF16)
    bcol_all = sum(jnp.dot(tril, p, preferred_element_type=F32) for p in _split3(gate))
    brow_all = sum(jnp.dot(p, triu, preferred_element_type=F32) for p in _split3(gatet))
    lane = lax.broadcasted_iota(jnp.int32, (L, LANES), 1)
    one_col = (lane == 0).astype(BF16)
    vall = v_ref[0]
    oall = o_ref[0].astype(F32)

    for hd in range(ML_HEADS):
        sl = slice(hd * ML_DH, (hd + 1) * ML_DH)
        q = qk[:, sl].astype(BF16)
        kf = qk[:, ML_W + hd * ML_DH:ML_W + (hd + 1) * ML_DH] * (ML_DH ** -0.5)
        vaug = jnp.concatenate([vall[:, sl], one_col], axis=1)
        ig_row = gatet[hd:hd + 1, :]
        ig_col = gate[:, hd:hd + 1]
        b_row = brow_all[ML_HEADS + hd:ML_HEADS + hd + 1, :]
        b_col = bcol_all[:, ML_HEADS + hd:ML_HEADS + hd + 1]
        g = b_col[L - 1:L, :]
        m_prev = m_ref[hd:hd + 1, 0:1]

        dmat = jnp.where(causal, b_col - b_row + ig_row, NEG_BIG)
        m_inter = b_col + m_prev
        m_t = jnp.maximum(m_inter, jnp.max(dmat, axis=1, keepdims=True))
        s = lax.dot_general(q, kf.astype(BF16), (((1,), (1,)), ((), ())),
                            preferred_element_type=F32)
        scores = (s * jnp.exp(dmat - m_t)).astype(BF16)
        inter = jnp.exp(m_inter - m_t)
        state = state_ref[hd]
        nd = (jnp.dot(scores, vaug, preferred_element_type=F32)
              + inter * jnp.dot(q, state.astype(BF16), preferred_element_type=F32))
        num = nd[:, 0:ML_DH]
        den = nd[:, ML_DH:ML_DH + 1]
        hv = num / jnp.maximum(jnp.abs(den), jnp.exp(-m_t))

        a_col = g - b_col + ig_col
        m_new = jnp.maximum(g + m_prev, jnp.max(a_col, axis=0, keepdims=True))
        decay = jnp.exp(g + m_prev - m_new)
        kw = (kf * jnp.exp(a_col - m_new)).astype(BF16)
        state_ref[hd] = decay * state + lax.dot_general(
            kw, vaug, (((0,), (0,)), ((), ())), preferred_element_type=F32)
        m_ref[hd:hd + 1, :] = jnp.broadcast_to(m_new, (1, LANES))

        hn = hv * lax.rsqrt(jnp.mean(hv * hv, axis=-1, keepdims=True) + EPS) * nrm_ref[:, sl]
        out_ref[0, :, sl] = (hn * jax.nn.sigmoid(oall[:, sl])).astype(BF16)


def _mlstm(qk, v, o, gate, gatet, conv_w, conv_b, nrm, *, L):
    B, S, _ = qk.shape
    kern = functools.partial(_mlstm_kernel, L=L)
    row_spec = lambda n: pl.BlockSpec((1, L, n), lambda b, c: (b, c, 0))
    return pl.pallas_call(
        kern,
        grid=(B, S // L),
        in_specs=[
            row_spec(2 * ML_W), row_spec(ML_W), row_spec(ML_W), row_spec(LANES),
            pl.BlockSpec((1, _N_GT, L), lambda b, c: (b, 0, c)),
            _const_spec((CONV_W, 2 * ML_W)),
            _const_spec((1, 2 * ML_W)),
            _const_spec((1, ML_W)),
        ],
        out_specs=row_spec(ML_W),
        out_shape=jax.ShapeDtypeStruct((B, S, ML_W), BF16),
        scratch_shapes=[
            pltpu.VMEM((L + _TAIL, 2 * ML_W), F32),
            pltpu.VMEM((ML_HEADS, ML_DH, 2 * ML_DH), F32),
            pltpu.VMEM((8, LANES), F32),
        ],
        compiler_params=pltpu.CompilerParams(
            dimension_semantics=("arbitrary", "arbitrary"), vmem_limit_bytes=VMEM_LIMIT),
        name="mlstm",
    )(qk, v, o, gate, gatet, conv_w, conv_b, nrm)


_HP = 2


def _fox_kernel(q_ref, k_ref, vt_ref, o_ref, *, t):
    qi = pl.program_id(2)
    r_i = lax.broadcasted_iota(jnp.int32, (t, t), 0)
    c_i = lax.broadcasted_iota(jnp.int32, (t, t), 1)

    def step(j, carry, diagonal):
        new = []
        for hd in range(_HP):
            m, l, acc = carry[hd]
            st = lax.dot_general(k_ref[0, hd, j], q_ref[0, hd], (((1,), (1,)), ((), ())),
                                 preferred_element_type=F32)
            if diagonal:
                st = jnp.where(c_i >= r_i, st, NEG_BIG)
            m_new = jnp.maximum(m, jnp.max(st, axis=0, keepdims=True))
            alpha = jnp.exp(m - m_new)
            p = jnp.exp(st - m_new)
            l = alpha * l + jnp.sum(p, axis=0, keepdims=True)
            vt = vt_ref[0, j, hd * FX_DH:(hd + 1) * FX_DH, :]
            acc = alpha * acc + jnp.dot(vt, p.astype(BF16), preferred_element_type=F32)
            new.append((m_new, l, acc))
        return tuple(new)

    init = tuple((jnp.full((1, t), NEG_BIG, F32), jnp.zeros((1, t), F32),
                  jnp.zeros((FX_DH, t), F32)) for _ in range(_HP))
    carry = lax.fori_loop(0, qi, lambda j, c: step(j, c, False), init)
    carry = step(qi, carry, True)
    out_t = jnp.concatenate([acc / l for (_, l, acc) in carry], axis=0)
    o_ref[0] = out_t.T.astype(BF16)


def _fox(fq, fk, fvt, *, t):
    B, H, S, _ = fq.shape
    n = S // t
    fk5 = fk.reshape(B, H, n, t, LANES)
    kern = functools.partial(_fox_kernel, t=t)
    return pl.pallas_call(
        kern,
        grid=(B, H // _HP, n),
        in_specs=[
            pl.BlockSpec((1, _HP, t, LANES), lambda b, h, q: (b, h, q, 0)),
            pl.BlockSpec((1, _HP, n, t, LANES), lambda b, h, q: (b, h, 0, 0, 0)),
            pl.BlockSpec((1, n, _HP * FX_DH, t), lambda b, h, q: (b, 0, h, 0)),
        ],
        out_specs=pl.BlockSpec((1, t, _HP * FX_DH), lambda b, h, q: (b, q, h)),
        out_shape=jax.ShapeDtypeStruct((B, S, FX_W), BF16),
        compiler_params=pltpu.CompilerParams(
            dimension_semantics=("arbitrary", "arbitrary", "arbitrary"),
            vmem_limit_bytes=VMEM_LIMIT),
        name="fox",
    )(fq, fk5, fvt)


def _memkv_kernel(mem_ref, g_ref, w_ref, k_ref, v_ref):
    mn = _rms(mem_ref[0], g_ref[...]).astype(BF16)
    k_ref[0] = jnp.dot(mn, w_ref[:, 0:D_MODEL], preferred_element_type=F32).astype(BF16)
    v_ref[0] = jnp.dot(mn, w_ref[:, D_MODEL:2 * D_MODEL], preferred_element_type=F32).astype(BF16)


def _memkv(mem, ln_mem, w_xkv):
    B, M, _ = mem.shape
    spec = pl.BlockSpec((1, M, D_MODEL), lambda b: (b, 0, 0))
    return pl.pallas_call(
        _memkv_kernel,
        grid=(B,),
        in_specs=[spec, _const_spec((1, D_MODEL)), _const_spec(w_xkv.shape)],
        out_specs=(spec, spec),
        out_shape=(jax.ShapeDtypeStruct((B, M, D_MODEL), BF16),) * 2,
        compiler_params=pltpu.CompilerParams(
            dimension_semantics=("arbitrary",), vmem_limit_bytes=VMEM_LIMIT),
        name="memkv",
    )(mem, ln_mem, w_xkv)


def _xattn_kernel(x_ref, ml_ref, fx_ref, wout_ref, lnx_ref, wq_ref, km_ref, vm_ref, wo_ref,
                  out_ref):
    x1 = (x_ref[0]
          + jnp.dot(ml_ref[0], wout_ref[0:ML_W, :], preferred_element_type=F32)
          + jnp.dot(fx_ref[0], wout_ref[ML_W:ML_W + FX_W, :], preferred_element_type=F32))
    hx = _rms(x1, lnx_ref[...]).astype(BF16)
    q = (jnp.dot(hx, wq_ref[...], preferred_element_type=F32) * (X_DH ** -0.5)).astype(BF16)
    km = km_ref[0]
    vm = vm_ref[0]
    heads = []
    for hd in range(X_HEADS):
        sl = slice(hd * X_DH, (hd + 1) * X_DH)
        s = lax.dot_general(q[:, sl], km[:, sl], (((1,), (1,)), ((), ())),
                            preferred_element_type=F32)
        p = jnp.exp(s - jnp.max(s, axis=-1, keepdims=True))
        l = jnp.sum(p, axis=-1, keepdims=True)
        oh = jnp.dot(p.astype(BF16), vm[:, sl], preferred_element_type=F32) / l
        heads.append(oh.astype(BF16))
    o = jnp.concatenate(heads, axis=1)
    out_ref[0] = x1 + jnp.dot(o, wo_ref[...], preferred_element_type=F32)


def _xattn(x, ml, fx, w_out, ln_x, w_xq, kmem, vmem, w_xo, *, tm):
    B, S, _ = x.shape
    M = kmem.shape[1]
    row_spec = lambda n: pl.BlockSpec((1, tm, n), lambda b, s: (b, s, 0))
    mem_spec = pl.BlockSpec((1, M, D_MODEL), lambda b, s: (b, 0, 0))
    return pl.pallas_call(
        _xattn_kernel,
        grid=(B, S // tm),
        in_specs=[
            row_spec(D_MODEL), row_spec(ML_W), row_spec(FX_W),
            _const_spec(w_out.shape), _const_spec((1, D_MODEL)), _const_spec(w_xq.shape),
            mem_spec, mem_spec, _const_spec(w_xo.shape),
        ],
        out_specs=row_spec(D_MODEL),
        out_shape=jax.ShapeDtypeStruct((B, S, D_MODEL), F32),
        compiler_params=pltpu.CompilerParams(
            dimension_semantics=("arbitrary", "arbitrary"), vmem_limit_bytes=VMEM_LIMIT),
        name="xattn",
    )(x, ml, fx, w_out, ln_x, w_xq, kmem, vmem, w_xo)


_FF_CHUNK = 1024


def _mlp_kernel(x_ref, ln2_ref, w1_ref, w2_ref, lnf_ref, out_ref):
    x = x_ref[0]
    h = _rms(x, ln2_ref[...]).astype(BF16)
    acc = x
    for c in range(D_FF // _FF_CHUNK):
        sl = slice(c * _FF_CHUNK, (c + 1) * _FF_CHUNK)
        u = jnp.maximum(jnp.dot(h, w1_ref[:, sl], preferred_element_type=F32), 0.0)
        acc = acc + jnp.dot((u * u).astype(BF16), w2_ref[sl, :], preferred_element_type=F32)
    out_ref[0] = _rms(acc, lnf_ref[...])


def _mlp(x, ln2, w1, w2, ln_f, *, tm):
    B, S, _ = x.shape
    row_spec = pl.BlockSpec((1, tm, D_MODEL), lambda b, s: (b, s, 0))
    return pl.pallas_call(
        _mlp_kernel,
        grid=(B, S // tm),
        in_specs=[row_spec, _const_spec((1, D_MODEL)), _const_spec(w1.shape),
                  _const_spec(w2.shape), _const_spec((1, D_MODEL))],
        out_specs=row_spec,
        out_shape=jax.ShapeDtypeStruct((B, S, D_MODEL), F32),
        compiler_params=pltpu.CompilerParams(
            dimension_semantics=("arbitrary", "arbitrary"), vmem_limit_bytes=VMEM_LIMIT),
        name="mlp",
    )(x, ln2, w1, w2, ln_f)


def _layer(x, mem, ln1, w_in, conv_w, conv_b, b_i, b_f, ml_norm, fx_b_f, w_out,
           ln_x, ln_mem, w_xq, w_xkv, w_xo, ln2, w_ff1, w_ff2, ln_f_or_none):
    B, S, _ = x.shape
    t = _tiles(S)
    o_qk, o_v, o_o = 0, 2 * ML_W, 3 * ML_W
    o_i = 4 * ML_W
    o_f = o_i + ML_HEADS
    o_fq = o_f + ML_HEADS
    o_fk, o_fv = o_fq + FX_W, o_fq + 2 * FX_W
    o_ff = o_fq + 3 * FX_W
    w_gate = jnp.concatenate(
        [w_in[:, o_i:o_fq], w_in[:, o_ff:o_ff + FX_HEADS]], axis=1)
    w_main = jnp.concatenate(
        [w_in[:, o_qk:o_i], w_in[:, o_fq:o_fv],
         jnp.pad(w_gate, ((0, 0), (0, LANES - _N_GT)))], axis=1).astype(BF16)
    w_t = jnp.concatenate([w_in[:, o_fv:o_ff], w_gate], axis=1).T.astype(BF16)
    bias = jnp.concatenate([b_i, b_f, fx_b_f]).astype(F32)
    brow = jnp.pad(bias, (0, LANES - _N_GT)).reshape(1, LANES)
    bcol = bias.reshape(_N_GT, 1)

    qk, v, o, gate, gatet, fq, fk, fvt = _proj(
        x, ln1.reshape(1, -1), w_main, w_t, brow, bcol, tm=t["tm"], tkv=t["tkv"])
    ml = _mlstm(qk, v, o, gate, gatet, conv_w, conv_b.reshape(1, -1), ml_norm.reshape(1, -1),
                L=t["chunk"])
    fx = _fox(fq, fk, fvt, t=t["tkv"])
    kmem, vmem = _memkv(mem, ln_mem.reshape(1, -1), w_xkv.astype(BF16))
    x2 = _xattn(x, ml, fx, w_out.astype(BF16), ln_x.reshape(1, -1), w_xq.astype(BF16),
                kmem, vmem, w_xo.astype(BF16), tm=t["tail"])
    return _mlp(x2, ln2.reshape(1, -1), w_ff1.astype(BF16), w_ff2.astype(BF16),
                ln_f_or_none.reshape(1, -1), tm=t["tail"])


def kernel(x, mem, ln1, w_in, ml_conv_w, ml_conv_b, ml_b_i, ml_b_f, ml_norm, fx_b_f, w_out,
           ln_x, ln_mem, w_xq, w_xkv, w_xo, ln2, w_ff1, w_ff2, ln_f):
    depth = w_in.shape[0]
    assert depth == 1, "single-layer problem: the final norm is fused into the layer's MLP kernel"
    return _layer(x, mem, ln1[0], w_in[0], ml_conv_w[0], ml_conv_b[0], ml_b_i[0], ml_b_f[0],
                  ml_norm[0], fx_b_f[0], w_out[0], ln_x[0], ln_mem[0], w_xq[0], w_xkv[0],
                  w_xo[0], ln2[0], w_ff1[0], w_ff2[0], ln_f)
```

```python
import functools

import jax
import jax.numpy as jnp
from jax import lax
from jax.experimental import pallas as pl
from jax.experimental.pallas import tpu as pltpu

F32 = jnp.float32
BF16 = jnp.bfloat16

D_MODEL = 1024
ML_HEADS = 4
ML_DH = 128
ML_W = ML_HEADS * ML_DH
CONV_W = 4
FX_HEADS = 8
FX_DH = 64
FX_W = FX_HEADS * FX_DH
X_HEADS = 4
X_DH = D_MODEL // X_HEADS
D_FF = 4 * D_MODEL
EPS = 1e-6

LANES = 128
NEG_BIG = -1e30
LOG2E = 1.4426950408889634
VMEM_LIMIT = 56 * 1024 * 1024

_C_QK = 0
_C_V = _C_QK + 2 * ML_W
_C_O = _C_V + ML_W
_C_FQ = _C_O + ML_W
_C_FK = _C_FQ + FX_W
_C_G = _C_FK + FX_W
_C_END = _C_G + LANES
_R_GT = FX_W
_N_GT = 16

_AUG0 = FX_DH


def _tiles(S):
    tm = min(512, S)
    return dict(
        tm=tm,
        chunk=min(256, S),
        tq=min(256, S),
        tkc=max(tm, min(2048, S)),
        tail=min(512, S),
    )


def _rms(x, g):
    return x * lax.rsqrt(jnp.mean(x * x, axis=-1, keepdims=True) + EPS) * g


def _split3(x):
    hi = x.astype(BF16)
    r = x - hi.astype(F32)
    mid = r.astype(BF16)
    lo = (r - mid.astype(F32)).astype(BF16)
    return hi, mid, lo


def _log_sigmoid(x):
    return jnp.minimum(x, 0.0) - jnp.log1p(jnp.exp(-jnp.abs(x)))


def _const_spec(shape):
    nd = len(shape)
    return pl.BlockSpec(shape, lambda *_: (0,) * nd, pipeline_mode=pl.Buffered(1))


def _proj_kernel(x_ref, g_ref, w_ref, wt_ref, brow_ref, bcol_ref,
                 qk_ref, v_ref, o_ref, gate_ref, gatet_ref, fq_ref, fk_ref, fvt_ref,
                 carry_ref, *, tm):
    @pl.when(pl.program_id(1) == 0)
    def _():
        carry_ref[...] = jnp.zeros_like(carry_ref)

    x = x_ref[0]
    h = _rms(x, g_ref[...]).astype(BF16)

    def proj(c0, c1):
        return jnp.dot(h, w_ref[:, c0:c1], preferred_element_type=F32)

    qk_ref[0] = proj(_C_QK, _C_V).astype(BF16)
    v_ref[0] = proj(_C_V, _C_O).astype(BF16)
    o_ref[0] = proj(_C_O, _C_FQ).astype(BF16)

    lane = lax.broadcasted_iota(jnp.int32, (tm, LANES), 1)
    gpre = proj(_C_G, _C_END) + brow_ref[...]
    gls = _log_sigmoid(gpre)
    gate_ref[0] = jnp.where(lane < ML_HEADS, gpre, gls)

    zt = lax.dot_general(wt_ref[...], h, (((1,), (1,)), ((), ())), preferred_element_type=F32)
    fvt_ref[0, 0] = zt[0:FX_W, :].astype(BF16)
    gt = zt[_R_GT:_R_GT + _N_GT, :] + bcol_ref[...]
    row = lax.broadcasted_iota(jnp.int32, (_N_GT, tm), 0)
    gatet_ref[0] = jnp.where(row < ML_HEADS, gt, _log_sigmoid(gt))

    r_i = lax.broadcasted_iota(jnp.int32, (tm, tm), 0)
    c_i = lax.broadcasted_iota(jnp.int32, (tm, tm), 1)
    tril = (c_i <= r_i).astype(BF16)
    csum = carry_ref[...] + sum(
        jnp.dot(tril, part, preferred_element_type=F32) for part in _split3(gls))
    carry_ref[...] = csum[tm - 1:tm, :]
    c_hi, c_mid, c_lo = (p.astype(F32) for p in _split3(csum * LOG2E))

    zq = proj(_C_FQ, _C_FK) * (FX_DH ** -0.5 * LOG2E)
    zk = proj(_C_FK, _C_G)
    ones3 = ((lane >= _AUG0 + 3) & (lane < _AUG0 + 6)).astype(F32)
    ones3k = ((lane >= _AUG0) & (lane < _AUG0 + 3)).astype(F32)
    for hd in range(FX_HEADS):
        grp = hd // 2
        zq_h = zq[:, grp * LANES:(grp + 1) * LANES]
        zk_h = zk[:, grp * LANES:(grp + 1) * LANES]
        if hd % 2:
            zq_h = pltpu.roll(zq_h, LANES // 2, 1)
            zk_h = pltpu.roll(zk_h, LANES // 2, 1)
        gl = 2 * ML_HEADS + hd
        hi_c, mid_c, lo_c = (c[:, gl:gl + 1] for c in (c_hi, c_mid, c_lo))
        qa = jnp.where(lane < _AUG0, zq_h,
             jnp.where(lane == _AUG0, hi_c,
             jnp.where(lane == _AUG0 + 1, mid_c,
             jnp.where(lane == _AUG0 + 2, lo_c, ones3))))
        ka = jnp.where(lane < _AUG0, zk_h,
             jnp.where(lane == _AUG0 + 3, -hi_c,
             jnp.where(lane == _AUG0 + 4, -mid_c,
             jnp.where(lane == _AUG0 + 5, -lo_c, ones3k))))
        fq_ref[0, hd] = qa.astype(BF16)
        fk_ref[0, hd] = ka.astype(BF16)


def _proj(x, ln1, w_main, w_t, brow, bcol, *, tm, tkc):
    B, S, _ = x.shape
    ns = S // tm
    r = tkc // tm
    kern = functools.partial(_proj_kernel, tm=tm)
    row_spec = lambda n: pl.BlockSpec((1, tm, n), lambda b, s: (b, s, 0))
    out_shape = (
        jax.ShapeDtypeStruct((B, S, 2 * ML_W), BF16),
        jax.ShapeDtypeStruct((B, S, ML_W), BF16),
        jax.ShapeDtypeStruct((B, S, ML_W), BF16),
        jax.ShapeDtypeStruct((B, S, LANES), F32),
        jax.ShapeDtypeStruct((B, _N_GT, S), F32),
        jax.ShapeDtypeStruct((B, FX_HEADS, S, LANES), BF16),
        jax.ShapeDtypeStruct((B, FX_HEADS, S, LANES), BF16),
        jax.ShapeDtypeStruct((B, S // tkc, FX_W, tkc), BF16),
    )
    out_specs = (
        row_spec(2 * ML_W), row_spec(ML_W), row_spec(ML_W), row_spec(LANES),
        pl.BlockSpec((1, _N_GT, tm), lambda b, s: (b, 0, s)),
        pl.BlockSpec((1, FX_HEADS, tm, LANES), lambda b, s: (b, 0, s, 0)),
        pl.BlockSpec((1, FX_HEADS, tm, LANES), lambda b, s: (b, 0, s, 0)),
        pl.BlockSpec((1, 1, FX_W, tm), lambda b, s: (b, s // r, 0, s % r)),
    )
    return pl.pallas_call(
        kern,
        grid=(B, ns),
        in_specs=[
            row_spec(D_MODEL),
            _const_spec((1, D_MODEL)),
            _const_spec(w_main.shape),
            _const_spec(w_t.shape),
            _const_spec((1, LANES)),
            _const_spec((_N_GT, 1)),
        ],
        out_specs=out_specs,
        out_shape=out_shape,
        scratch_shapes=[pltpu.VMEM((1, LANES), F32)],
        compiler_params=pltpu.CompilerParams(
            dimension_semantics=("arbitrary", "arbitrary"), vmem_limit_bytes=VMEM_LIMIT),
        name="proj",
    )(x, ln1, w_main, w_t, brow, bcol)


_TAIL = 8


def _mlstm_kernel(qk_ref, v_ref, o_ref, gate_ref, gatet_ref, cw_ref, cb_ref, nrm_ref,
                  out_ref, xbuf_ref, state_ref, m_ref, *, L):
    @pl.when(pl.program_id(1) == 0)
    def _():
        xbuf_ref[0:_TAIL, :] = jnp.zeros((_TAIL, 2 * ML_W), F32)
        state_ref[...] = jnp.zeros_like(state_ref)
        m_ref[...] = jnp.zeros_like(m_ref)

    xbuf_ref[_TAIL:_TAIL + L, :] = qk_ref[0].astype(F32)
    conv = cb_ref[...]
    for j in range(CONV_W):
        off = _TAIL - (CONV_W - 1) + j
        conv = conv + xbuf_ref[off:off + L, :] * cw_ref[j:j + 1, :]
    xbuf_ref[0:_TAIL, :] = xbuf_ref[L:L + _TAIL, :]
    qk = conv * jax.nn.sigmoid(conv)

    gate = gate_ref[0]
    gatet = gatet_ref[0]
    r_i = lax.broadcasted_iota(jnp.int32, (L, L), 0)
    c_i = lax.broadcasted_iota(jnp.int32, (L, L), 1)
    causal = c_i <= r_i
    tril = causal.astype(BF16)
    triu = (r_i <= c_i).astype(BF16)
    bcol_all = sum(jnp.dot(tril, p, preferred_element_type=F32) for p in _split3(gate))
    brow_all = sum(jnp.dot(p, triu, preferred_element_type=F32) for p in _split3(gatet))
    lane = lax.broadcasted_iota(jnp.int32, (L, LANES), 1)
    one_col = (lane == 0).astype(BF16)
    vall = v_ref[0]
    oall = o_ref[0].astype(F32)

    for hd in range(ML_HEADS):
        sl = slice(hd * ML_DH, (hd + 1) * ML_DH)
        q = qk[:, sl].astype(BF16)
        kf = qk[:, ML_W + hd * ML_DH:ML_W + (hd + 1) * ML_DH] * (ML_DH ** -0.5)
        vaug = jnp.concatenate([vall[:, sl], one_col], axis=1)
        ig_row = gatet[hd:hd + 1, :]
        ig_col = gate[:, hd:hd + 1]
        b_row = brow_all[ML_HEADS + hd:ML_HEADS + hd + 1, :]
        b_col = bcol_all[:, ML_HEADS + hd:ML_HEADS + hd + 1]
        g = b_col[L - 1:L, :]
        m_prev = m_ref[hd:hd + 1, 0:1]

        dmat = jnp.where(causal, b_col - b_row + ig_row, NEG_BIG)
        m_inter = b_col + m_prev
        m_t = jnp.maximum(m_inter, jnp.max(dmat, axis=1, keepdims=True))
        s = lax.dot_general(q, kf.astype(BF16), (((1,), (1,)), ((), ())),
                            preferred_element_type=F32)
        scores = (s * jnp.exp(dmat - m_t)).astype(BF16)
        inter = jnp.exp(m_inter - m_t)
        state = state_ref[hd]
        nd = (jnp.dot(scores, vaug, preferred_element_type=F32)
              + inter * jnp.dot(q, state.astype(BF16), preferred_element_type=F32))
        num = nd[:, 0:ML_DH]
        den = nd[:, ML_DH:ML_DH + 1]
        hv = num / jnp.maximum(jnp.abs(den), jnp.exp(-m_t))

        a_col = g - b_col + ig_col
        m_new = jnp.maximum(g + m_prev, jnp.max(a_col, axis=0, keepdims=True))
        decay = jnp.exp(g + m_prev - m_new)
        kw = (kf * jnp.exp(a_col - m_new)).astype(BF16)
        state_ref[hd] = decay * state + lax.dot_general(
            kw, vaug, (((0,), (0,)), ((), ())), preferred_element_type=F32)
        m_ref[hd:hd + 1, :] = jnp.broadcast_to(m_new, (1, LANES))

        hn = hv * lax.rsqrt(jnp.mean(hv * hv, axis=-1, keepdims=True) + EPS) * nrm_ref[:, sl]
        out_ref[0, :, sl] = (hn * jax.nn.sigmoid(oall[:, sl])).astype(BF16)


def _mlstm(qk, v, o, gate, gatet, conv_w, conv_b, nrm, *, L):
    B, S, _ = qk.shape
    kern = functools.partial(_mlstm_kernel, L=L)
    row_spec = lambda n: pl.BlockSpec((1, L, n), lambda b, c: (b, c, 0))
    return pl.pallas_call(
        kern,
        grid=(B, S // L),
        in_specs=[
            row_spec(2 * ML_W), row_spec(ML_W), row_spec(ML_W), row_spec(LANES),
            pl.BlockSpec((1, _N_GT, L), lambda b, c: (b, 0, c)),
            _const_spec((CONV_W, 2 * ML_W)),
            _const_spec((1, 2 * ML_W)),
            _const_spec((1, ML_W)),
        ],
        out_specs=row_spec(ML_W),
        out_shape=jax.ShapeDtypeStruct((B, S, ML_W), BF16),
        scratch_shapes=[
            pltpu.VMEM((L + _TAIL, 2 * ML_W), F32),
            pltpu.VMEM((ML_HEADS, ML_DH, 2 * ML_DH), F32),
            pltpu.VMEM((8, LANES), F32),
        ],
        compiler_params=pltpu.CompilerParams(
            dimension_semantics=("arbitrary", "arbitrary"), vmem_limit_bytes=VMEM_LIMIT),
        name="mlstm",
    )(qk, v, o, gate, gatet, conv_w, conv_b, nrm)


_HP = 4
_VROWS = FX_DH + 16


def _fox_kernel(q_ref, k_ref, vt_ref, o_ref, m_ref, acc_ref, *, tq, tkc):
    qi = pl.program_id(2)
    nsub = tkc // tq
    n_full = qi // nsub
    rem = qi % nsub
    key_i = lax.broadcasted_iota(jnp.int32, (tq, tq), 0)
    qry_i = lax.broadcasted_iota(jnp.int32, (tq, tq), 1)

    m_ref[...] = jnp.full_like(m_ref, NEG_BIG)
    acc_ref[...] = jnp.zeros_like(acc_ref)
    ones_rows = (lax.broadcasted_iota(jnp.int32, (_VROWS - FX_DH, tq), 0) == 0).astype(BF16)

    def chunk(j, n_plain, diagonal):
        state = [(m_ref[hd], acc_ref[hd]) for hd in range(_HP)]
        n_sub = n_plain + int(diagonal)
        scores = {}
        for i in range(n_sub):
            ks = slice(i * tq, (i + 1) * tq)
            for hd in range(_HP):
                st = lax.dot_general(k_ref[0, hd, j, ks, :], q_ref[0, hd],
                                     (((1,), (1,)), ((), ())),
                                     preferred_element_type=F32)
                if diagonal and i == n_plain:
                    st = jnp.where(qry_i >= key_i, st, NEG_BIG)
                scores[i, hd] = st
        for i in range(n_sub):
            ks = slice(i * tq, (i + 1) * tq)
            for hd in range(_HP):
                m, acc = state[hd]
                st = scores[i, hd]
                m_new = jnp.maximum(m, jnp.max(st, axis=0, keepdims=True))
                alpha = jnp.exp2(m - m_new)
                p = jnp.exp2(st - m_new).astype(BF16)
                vt = jnp.concatenate(
                    [vt_ref[0, j, hd * FX_DH:(hd + 1) * FX_DH, ks], ones_rows], axis=0)
                acc = alpha * acc + jnp.dot(vt, p, preferred_element_type=F32)
                state[hd] = (m_new, acc)
        for hd in range(_HP):
            m_ref[hd], acc_ref[hd] = state[hd]

    def body(j, _):
        chunk(j, nsub, False)
        return 0

    lax.fori_loop(0, n_full, body, 0)
    for r in range(nsub):
        pl.when(rem == r)(functools.partial(chunk, n_full, r, True))
    out_t = jnp.concatenate(
        [acc_ref[hd, 0:FX_DH, :] / acc_ref[hd, FX_DH:FX_DH + 1, :] for hd in range(_HP)],
        axis=0)
    o_ref[0] = out_t.T.astype(BF16)


def _fox(fq, fk, fvt, *, tq, tkc):
    B, H, S, _ = fq.shape
    n = S // tkc
    fk5 = fk.reshape(B, H, n, tkc, LANES)
    kern = functools.partial(_fox_kernel, tq=tq, tkc=tkc)
    return pl.pallas_call(
        kern,
        grid=(B, H // _HP, S // tq),
        in_specs=[
            pl.BlockSpec((1, _HP, tq, LANES), lambda b, h, q: (b, h, q, 0)),
            pl.BlockSpec((1, _HP, n, tkc, LANES), lambda b, h, q: (b, h, 0, 0, 0)),
            pl.BlockSpec((1, n, _HP * FX_DH, tkc), lambda b, h, q: (b, 0, h, 0)),
        ],
        out_specs=pl.BlockSpec((1, tq, _HP * FX_DH), lambda b, h, q: (b, q, h)),
        out_shape=jax.ShapeDtypeStruct((B, S, FX_W), BF16),
        scratch_shapes=[
            pltpu.VMEM((_HP, 1, tq), F32),
            pltpu.VMEM((_HP, _VROWS, tq), F32),
        ],
        compiler_params=pltpu.CompilerParams(
            dimension_semantics=("arbitrary", "arbitrary", "arbitrary"),
            vmem_limit_bytes=VMEM_LIMIT),
        name="fox",
    )(fq, fk5, fvt)


def _memkv_kernel(mem_ref, g_ref, w_ref, k_ref, v_ref):
    mn = _rms(mem_ref[0], g_ref[...]).astype(BF16)
    k_ref[0] = jnp.dot(mn, w_ref[:, 0:D_MODEL], preferred_element_type=F32).astype(BF16)
    v_ref[0] = jnp.dot(mn, w_ref[:, D_MODEL:2 * D_MODEL], preferred_element_type=F32).astype(BF16)


def _memkv(mem, ln_mem, w_xkv):
    B, M, _ = mem.shape
    spec = pl.BlockSpec((1, M, D_MODEL), lambda b: (b, 0, 0))
    return pl.pallas_call(
        _memkv_kernel,
        grid=(B,),
        in_specs=[spec, _const_spec((1, D_MODEL)), _const_spec(w_xkv.shape)],
        out_specs=(spec, spec),
        out_shape=(jax.ShapeDtypeStruct((B, M, D_MODEL), BF16),) * 2,
        compiler_params=pltpu.CompilerParams(
            dimension_semantics=("arbitrary",), vmem_limit_bytes=VMEM_LIMIT),
        name="memkv",
    )(mem, ln_mem, w_xkv)


def _xattn_kernel(x_ref, ml_ref, fx_ref, wout_ref, lnx_ref, wq_ref, km_ref, vm_ref, wo_ref,
                  out_ref):
    x1 = (x_ref[0]
          + jnp.dot(ml_ref[0], wout_ref[0:ML_W, :], preferred_element_type=F32)
          + jnp.dot(fx_ref[0], wout_ref[ML_W:ML_W + FX_W, :], preferred_element_type=F32))
    hx = _rms(x1, lnx_ref[...]).astype(BF16)
    q = (jnp.dot(hx, wq_ref[...], preferred_element_type=F32) * (X_DH ** -0.5)).astype(BF16)
    km = km_ref[0]
    vm = vm_ref[0]
    heads = []
    for hd in range(X_HEADS):
        sl = slice(hd * X_DH, (hd + 1) * X_DH)
        s = lax.dot_general(q[:, sl], km[:, sl], (((1,), (1,)), ((), ())),
                            preferred_element_type=F32)
        p = jnp.exp(s - jnp.max(s, axis=-1, keepdims=True))
        l = jnp.sum(p, axis=-1, keepdims=True)
        oh = jnp.dot(p.astype(BF16), vm[:, sl], preferred_element_type=F32) / l
        heads.append(oh.astype(BF16))
    o = jnp.concatenate(heads, axis=1)
    out_ref[0] = x1 + jnp.dot(o, wo_ref[...], preferred_element_type=F32)


def _xattn(x, ml, fx, w_out, ln_x, w_xq, kmem, vmem, w_xo, *, tm):
    B, S, _ = x.shape
    M = kmem.shape[1]
    row_spec = lambda n: pl.BlockSpec((1, tm, n), lambda b, s: (b, s, 0))
    mem_spec = pl.BlockSpec((1, M, D_MODEL), lambda b, s: (b, 0, 0))
    return pl.pallas_call(
        _xattn_kernel,
        grid=(B, S // tm),
        in_specs=[
            row_spec(D_MODEL), row_spec(ML_W), row_spec(FX_W),
            _const_spec(w_out.shape), _const_spec((1, D_MODEL)), _const_spec(w_xq.shape),
            mem_spec, mem_spec, _const_spec(w_xo.shape),
        ],
        out_specs=row_spec(D_MODEL),
        out_shape=jax.ShapeDtypeStruct((B, S, D_MODEL), F32),
        compiler_params=pltpu.CompilerParams(
            dimension_semantics=("arbitrary", "arbitrary"), vmem_limit_bytes=VMEM_LIMIT),
        name="xattn",
    )(x, ml, fx, w_out, ln_x, w_xq, kmem, vmem, w_xo)


_FF_CHUNK = 1024


def _mlp_kernel(x_ref, ln2_ref, w1_ref, w2_ref, lnf_ref, out_ref):
    x = x_ref[0]
    h = _rms(x, ln2_ref[...]).astype(BF16)
    acc = x
    for c in range(D_FF // _FF_CHUNK):
        sl = slice(c * _FF_CHUNK, (c + 1) * _FF_CHUNK)
        u = jnp.maximum(jnp.dot(h, w1_ref[:, sl], preferred_element_type=F32), 0.0)
        acc = acc + jnp.dot((u * u).astype(BF16), w2_ref[sl, :], preferred_element_type=F32)
    out_ref[0] = _rms(acc, lnf_ref[...])


def _mlp(x, ln2, w1, w2, ln_f, *, tm):
    B, S, _ = x.shape
    row_spec = pl.BlockSpec((1, tm, D_MODEL), lambda b, s: (b, s, 0))
    return pl.pallas_call(
        _mlp_kernel,
        grid=(B, S // tm),
        in_specs=[row_spec, _const_spec((1, D_MODEL)), _const_spec(w1.shape),
                  _const_spec(w2.shape), _const_spec((1, D_MODEL))],
        out_specs=row_spec,
        out_shape=jax.ShapeDtypeStruct((B, S, D_MODEL), F32),
        compiler_params=pltpu.CompilerParams(
            dimension_semantics=("arbitrary", "arbitrary"), vmem_limit_bytes=VMEM_LIMIT),
        name="mlp",
    )(x, ln2, w1, w2, ln_f)


def _layer(x, mem, ln1, w_in, conv_w, conv_b, b_i, b_f, ml_norm, fx_b_f, w_out,
           ln_x, ln_mem, w_xq, w_xkv, w_xo, ln2, w_ff1, w_ff2, ln_f_or_none):
    B, S, _ = x.shape
    t = _tiles(S)
    o_qk, o_v, o_o = 0, 2 * ML_W, 3 * ML_W
    o_i = 4 * ML_W
    o_f = o_i + ML_HEADS
    o_fq = o_f + ML_HEADS
    o_fk, o_fv = o_fq + FX_W, o_fq + 2 * FX_W
    o_ff = o_fq + 3 * FX_W
    w_gate = jnp.concatenate(
        [w_in[:, o_i:o_fq], w_in[:, o_ff:o_ff + FX_HEADS]], axis=1)
    w_main = jnp.concatenate(
        [w_in[:, o_qk:o_i], w_in[:, o_fq:o_fv],
         jnp.pad(w_gate, ((0, 0), (0, LANES - _N_GT)))], axis=1).astype(BF16)
    w_t = jnp.concatenate([w_in[:, o_fv:o_ff], w_gate], axis=1).T.astype(BF16)
    bias = jnp.concatenate([b_i, b_f, fx_b_f]).astype(F32)
    brow = jnp.pad(bias, (0, LANES - _N_GT)).reshape(1, LANES)
    bcol = bias.reshape(_N_GT, 1)

    qk, v, o, gate, gatet, fq, fk, fvt = _proj(
        x, ln1.reshape(1, -1), w_main, w_t, brow, bcol, tm=t["tm"], tkc=t["tkc"])
    ml = _mlstm(qk, v, o, gate, gatet, conv_w, conv_b.reshape(1, -1), ml_norm.reshape(1, -1),
                L=t["chunk"])
    fx = _fox(fq, fk, fvt, tq=t["tq"], tkc=t["tkc"])
    kmem, vmem = _memkv(mem, ln_mem.reshape(1, -1), w_xkv.astype(BF16))
    x2 = _xattn(x, ml, fx, w_out.astype(BF16), ln_x.reshape(1, -1), w_xq.astype(BF16),
                kmem, vmem, w_xo.astype(BF16), tm=t["tail"])
    return _mlp(x2, ln2.reshape(1, -1), w_ff1.astype(BF16), w_ff2.astype(BF16),
                ln_f_or_none.reshape(1, -1), tm=t["tail"])


def kernel(x, mem, ln1, w_in, ml_conv_w, ml_conv_b, ml_b_i, ml_b_f, ml_norm, fx_b_f, w_out,
           ln_x, ln_mem, w_xq, w_xkv, w_xo, ln2, w_ff1, w_ff2, ln_f):
    depth = w_in.shape[0]
    assert depth == 1, "single-layer problem: the final norm is fused into the layer's MLP kernel"
    return _layer(x, mem, ln1[0], w_in[0], ml_conv_w[0], ml_conv_b[0], ml_b_i[0], ml_b_f[0],
                  ml_norm[0], fx_b_f[0], w_out[0], ln_x[0], ln_mem[0], w_xq[0], w_xkv[0],
                  w_xo[0], ln2[0], w_ff1[0], w_ff2[0], ln_f)
```

```python
import functools

import jax
import jax.numpy as jnp
from jax import lax
from jax.experimental import pallas as pl
from jax.experimental.pallas import tpu as pltpu

F32 = jnp.float32
BF16 = jnp.bfloat16

D_MODEL = 1024
ML_HEADS = 4
ML_DH = 128
ML_W = ML_HEADS * ML_DH
CONV_W = 4
FX_HEADS = 8
FX_DH = 64
FX_W = FX_HEADS * FX_DH
X_HEADS = 4
X_DH = D_MODEL // X_HEADS
D_FF = 4 * D_MODEL
EPS = 1e-6

LANES = 128
NEG_BIG = -1e30
LOG2E = 1.4426950408889634
_TAIL = 8
VMEM_LIMIT = 56 * 1024 * 1024

_C_QK = 0
_C_V = _C_QK + 2 * ML_W
_C_O = _C_V + ML_W
_C_FQ = _C_O + ML_W
_C_FK = _C_FQ + FX_W
_C_G = _C_FK + FX_W
_C_END = _C_G + LANES
_R_GT = FX_W
_N_GT = 16

_AUG0 = FX_DH


def _tiles(S):
    tm = min(512, S)
    return dict(
        tm=tm,
        chunk=min(256, S),
        tq=min(256, S),
        tkc=max(tm, min(2048, S)),
        tail=min(512, S),
    )


def _rms(x, g):
    return x * lax.rsqrt(jnp.mean(x * x, axis=-1, keepdims=True) + EPS) * g


def _split3(x):
    hi = x.astype(BF16)
    r = x - hi.astype(F32)
    mid = r.astype(BF16)
    lo = (r - mid.astype(F32)).astype(BF16)
    return hi, mid, lo


def _log_sigmoid(x):
    return jnp.minimum(x, 0.0) - jnp.log1p(jnp.exp(-jnp.abs(x)))


def _const_spec(shape):
    nd = len(shape)
    return pl.BlockSpec(shape, lambda *_: (0,) * nd, pipeline_mode=pl.Buffered(1))


def _proj_kernel(x_ref, g_ref, w_ref, wt_ref, brow_ref, bcol_ref, cw_ref, cb_ref,
                 mq_ref, mk_ref, v_ref, o_ref, gate_ref, gatet_ref, fq_ref, fk_ref, fvt_ref,
                 carry_ref, zbuf_ref, *, tm, L):
    @pl.when(pl.program_id(1) == 0)
    def _():
        carry_ref[...] = jnp.zeros_like(carry_ref)
        zbuf_ref[0:_TAIL, :] = jnp.zeros((_TAIL, 2 * ML_W), F32)

    x = x_ref[0]
    h = _rms(x, g_ref[...]).astype(BF16)

    def proj(c0, c1):
        return jnp.dot(h, w_ref[:, c0:c1], preferred_element_type=F32)

    lane = lax.broadcasted_iota(jnp.int32, (tm, LANES), 1)

    def in_lanes(lo, n):
        return (lane >= lo) & (lane < lo + n)

    gpre = proj(_C_G, _C_END) + brow_ref[...]
    gls = _log_sigmoid(gpre)
    r_i = lax.broadcasted_iota(jnp.int32, (tm, tm), 0)
    c_i = lax.broadcasted_iota(jnp.int32, (tm, tm), 1)
    tril = (c_i <= r_i).astype(BF16)
    carry = carry_ref[...]
    csum = carry + sum(jnp.dot(tril, part, preferred_element_type=F32) for part in _split3(gls))
    carry_ref[...] = csum[tm - 1:tm, :]
    is_ig = lax.broadcasted_iota(jnp.int32, (L, LANES), 1) < ML_HEADS
    for ci in range(tm // L):
        base = carry if ci == 0 else csum[ci * L - 1:ci * L, :]
        rows = slice(ci * L, (ci + 1) * L)
        gate_ref[0, rows, :] = LOG2E * jnp.where(is_ig, gpre[rows], csum[rows] - base)

    zq = proj(_C_FQ, _C_FK) * (FX_DH ** -0.5 * LOG2E)
    zk = proj(_C_FK, _C_G)
    c_hi, c_mid, c_lo = (p.astype(F32) for p in _split3(csum * LOG2E))
    src = 2 * ML_HEADS
    cq = jnp.where(in_lanes(_AUG0, 8), pltpu.roll(c_hi, _AUG0 - src, 1),
         jnp.where(in_lanes(_AUG0 + 8, 8), pltpu.roll(c_mid, _AUG0 + 8 - src, 1),
         jnp.where(in_lanes(_AUG0 + 16, 8), pltpu.roll(c_lo, _AUG0 + 16 - src, 1), 0.0)))
    ck = -pltpu.roll(cq, 24, 1)
    lane_row = lax.broadcasted_iota(jnp.int32, (1, LANES), 1)
    for hd in range(FX_HEADS):
        grp = hd // 2
        zq_h = zq[:, grp * LANES:(grp + 1) * LANES]
        zk_h = zk[:, grp * LANES:(grp + 1) * LANES]
        if hd % 2:
            zq_h = pltpu.roll(zq_h, LANES // 2, 1)
            zk_h = pltpu.roll(zk_h, LANES // 2, 1)
        hot = [(lane_row == base0 + 8 * j + hd) for j in range(3) for base0 in (_AUG0, _AUG0 + 24)]
        hot_k = (hot[0] | hot[2] | hot[4]).astype(F32)
        hot_q = (hot[1] | hot[3] | hot[5]).astype(F32)
        fq_ref[0, hd] = jnp.where(lane < _AUG0, zq_h, cq + hot_q).astype(BF16)
        fk_ref[0, hd] = jnp.where(lane < _AUG0, zk_h, ck + hot_k).astype(BF16)

    zqk = proj(_C_QK, _C_V)
    zbuf_ref[_TAIL:_TAIL + tm, :] = zqk
    conv = cb_ref[...] + zqk * cw_ref[CONV_W - 1:CONV_W, :]
    for j in range(CONV_W - 1):
        off = _TAIL - (CONV_W - 1) + j
        conv = conv + zbuf_ref[off:off + tm, :] * cw_ref[j:j + 1, :]
    zbuf_ref[0:_TAIL, :] = zbuf_ref[tm:tm + _TAIL, :]
    act = conv * jax.nn.sigmoid(conv)
    mq_ref[0] = act[:, 0:ML_W].astype(BF16)
    mk_ref[0] = (act[:, ML_W:2 * ML_W] * (ML_DH ** -0.5)).astype(BF16)

    zt = lax.dot_general(wt_ref[...], h, (((1,), (1,)), ((), ())), preferred_element_type=F32)
    fvt_ref[0, 0] = zt[0:FX_W, :].astype(BF16)
    gt = zt[_R_GT:_R_GT + _N_GT, :] + bcol_ref[...]
    glt = _log_sigmoid(gt)
    row = lax.broadcasted_iota(jnp.int32, (_N_GT, L), 0)
    triu = (lax.broadcasted_iota(jnp.int32, (L, L), 0)
            <= lax.broadcasted_iota(jnp.int32, (L, L), 1)).astype(BF16)
    for ci in range(tm // L):
        cols = slice(ci * L, (ci + 1) * L)
        brow = sum(jnp.dot(part[:, cols], triu, preferred_element_type=F32)
                   for part in _split3(glt))
        gatet_ref[0, :, cols] = LOG2E * jnp.where(row < ML_HEADS, gt[:, cols], brow)

    v_ref[0] = proj(_C_V, _C_O).astype(BF16)
    o_ref[0] = proj(_C_O, _C_FQ).astype(BF16)


def _proj(x, ln1, w_main, w_t, brow, bcol, conv_w, conv_b, *, tm, tkc, L):
    B, S, _ = x.shape
    ns = S // tm
    r = tkc // tm
    kern = functools.partial(_proj_kernel, tm=tm, L=L)
    row_spec = lambda n: pl.BlockSpec((1, tm, n), lambda b, s: (b, s, 0))
    out_shape = (
        jax.ShapeDtypeStruct((B, S, ML_W), BF16),
        jax.ShapeDtypeStruct((B, S, ML_W), BF16),
        jax.ShapeDtypeStruct((B, S, ML_W), BF16),
        jax.ShapeDtypeStruct((B, S, ML_W), BF16),
        jax.ShapeDtypeStruct((B, S, LANES), F32),
        jax.ShapeDtypeStruct((B, _N_GT, S), F32),
        jax.ShapeDtypeStruct((B, FX_HEADS, S, LANES), BF16),
        jax.ShapeDtypeStruct((B, FX_HEADS, S, LANES), BF16),
        jax.ShapeDtypeStruct((B, S // tkc, FX_W, tkc), BF16),
    )
    out_specs = (
        row_spec(ML_W), row_spec(ML_W), row_spec(ML_W), row_spec(ML_W), row_spec(LANES),
        pl.BlockSpec((1, _N_GT, tm), lambda b, s: (b, 0, s)),
        pl.BlockSpec((1, FX_HEADS, tm, LANES), lambda b, s: (b, 0, s, 0)),
        pl.BlockSpec((1, FX_HEADS, tm, LANES), lambda b, s: (b, 0, s, 0)),
        pl.BlockSpec((1, 1, FX_W, tm), lambda b, s: (b, s // r, 0, s % r)),
    )
    return pl.pallas_call(
        kern,
        grid=(B, ns),
        in_specs=[
            row_spec(D_MODEL),
            _const_spec((1, D_MODEL)),
            _const_spec(w_main.shape),
            _const_spec(w_t.shape),
            _const_spec((1, LANES)),
            _const_spec((_N_GT, 1)),
            _const_spec((CONV_W, 2 * ML_W)),
            _const_spec((1, 2 * ML_W)),
        ],
        out_specs=out_specs,
        out_shape=out_shape,
        scratch_shapes=[pltpu.VMEM((1, LANES), F32),
                        pltpu.VMEM((tm + _TAIL, 2 * ML_W), F32)],
        compiler_params=pltpu.CompilerParams(
            dimension_semantics=("arbitrary", "arbitrary"), vmem_limit_bytes=VMEM_LIMIT),
        name="proj",
    )(x, ln1, w_main, w_t, brow, bcol, conv_w, conv_b)


_ML_SL = [slice(hd * ML_DH, (hd + 1) * ML_DH) for hd in range(ML_HEADS)]
_V_INTER, _V_EM, _V_WA, _V_DECAY = 0, 1, 2, 3


def _mlstm_gates(gate, gatet, m_ref, et_ref, vec_ref, slot, *, L):
    src_i = lax.broadcasted_iota(jnp.int32, (L, L), 0)
    tgt_i = lax.broadcasted_iota(jnp.int32, (L, L), 1)
    causal_t = src_i <= tgt_i
    for hd in range(ML_HEADS):
        ig_row = gatet[hd:hd + 1, :]
        b_row = gatet[ML_HEADS + hd:ML_HEADS + hd + 1, :]
        r_col = gate[:, hd:hd + 1] - gate[:, ML_HEADS + hd:ML_HEADS + hd + 1]
        g = b_row[:, L - 1:L]
        m_prev = m_ref[hd:hd + 1, 0:1]
        dmat_t = jnp.where(causal_t, b_row + r_col, NEG_BIG)
        m_inter = b_row + m_prev
        m_t = jnp.maximum(m_inter, jnp.max(dmat_t, axis=0, keepdims=True))
        a_row = g - b_row + ig_row
        m_new = jnp.maximum(g + m_prev, jnp.max(a_row, axis=1, keepdims=True))
        m_ref[hd:hd + 1, :] = jnp.broadcast_to(m_new, (1, LANES))
        et_ref[slot, hd] = jnp.exp2(dmat_t - m_t)
        vec_ref[slot, hd, _V_INTER:_V_INTER + 1, :] = jnp.exp2(m_inter - m_t)
        vec_ref[slot, hd, _V_EM:_V_EM + 1, :] = jnp.exp2(-m_t)
        vec_ref[slot, hd, _V_WA:_V_WA + 1, :] = jnp.exp2(a_row - m_new)
        vec_ref[slot, hd, _V_DECAY:_V_DECAY + 1, :] = jnp.broadcast_to(
            jnp.exp2(g + m_prev - m_new), (1, L))


def _mlstm_mix(q_ref, k_ref, v_ref, et_ref, vec_ref, state_ref, hvt_ref, slot, *, L):
    one_rows = (lax.broadcasted_iota(jnp.int32, (ML_DH, L), 0) == 0).astype(BF16)
    heads = range(ML_HEADS)
    vec = [vec_ref[slot, hd] for hd in heads]
    q = [q_ref[0, :, _ML_SL[hd]] for hd in heads]
    k = [k_ref[0, :, _ML_SL[hd]] for hd in heads]
    vaug_t = [jnp.concatenate([v_ref[0, :, _ML_SL[hd]].T, one_rows], axis=0)
              for hd in heads]
    vaug_tw = [(vaug_t[hd].astype(F32) * vec[hd][_V_WA:_V_WA + 1, :]).astype(BF16) for hd in heads]
    s_t = [lax.dot_general(k[hd], q[hd], (((1,), (1,)), ((), ())),
                           preferred_element_type=F32) for hd in heads]
    qs_t = [lax.dot_general(state_ref[hd].astype(BF16), q[hd], (((1,), (1,)), ((), ())),
                            preferred_element_type=F32) for hd in heads]
    upd = [jnp.dot(vaug_tw[hd], k[hd], preferred_element_type=F32) for hd in heads]
    for hd in heads:
        state_ref[hd] = vec[hd][_V_DECAY:_V_DECAY + 1, 0:1] * state_ref[hd] + upd[hd]
    scores_t = [(s_t[hd] * et_ref[slot, hd]).astype(BF16) for hd in heads]
    nd_t = [jnp.dot(vaug_t[hd], scores_t[hd], preferred_element_type=F32)
            + vec[hd][_V_INTER:_V_INTER + 1, :] * qs_t[hd] for hd in heads]
    for hd in heads:
        den = nd_t[hd][ML_DH:ML_DH + 1, :]
        hvt_ref[slot, hd] = nd_t[hd][0:ML_DH, :] / jnp.maximum(
            jnp.abs(den), vec[hd][_V_EM:_V_EM + 1, :])


def _mlstm_out(hvt_ref, o_ref, nrm_ref, out_ref, slot):
    for hd in range(ML_HEADS):
        hv_t = hvt_ref[slot, hd]
        hn_t = hv_t * lax.rsqrt(jnp.mean(hv_t * hv_t, axis=0, keepdims=True) + EPS)
        og = jax.nn.sigmoid(o_ref[0, :, _ML_SL[hd]].astype(F32))
        out_ref[0, :, _ML_SL[hd]] = (hn_t.T * nrm_ref[:, _ML_SL[hd]] * og).astype(BF16)


def _mlstm_kernel(q_ref, k_ref, v_ref, o_ref, gate0_ref, gatet0_ref, gate_ref, gatet_ref, nrm_ref,
                  out_ref, state_ref, m_ref, e_ref, vec_ref, hv_ref, *, L):
    c = pl.program_id(1)

    @pl.when(c == 0)
    def _():
        state_ref[...] = jnp.zeros_like(state_ref)
        m_ref[...] = jnp.zeros_like(m_ref)
        hv_ref[...] = jnp.zeros_like(hv_ref)
        _mlstm_gates(gate0_ref[0], gatet0_ref[0], m_ref, e_ref, vec_ref, 0, L=L)

    slot = c % 2
    _mlstm_out(hv_ref, o_ref, nrm_ref, out_ref, 1 - slot)
    _mlstm_mix(q_ref, k_ref, v_ref, e_ref, vec_ref, state_ref, hv_ref, slot, L=L)
    _mlstm_gates(gate_ref[0], gatet_ref[0], m_ref, e_ref, vec_ref, 1 - slot, L=L)


def _mlstm(q, k, v, o, gate, gatet, nrm, *, L):
    B, S, _ = q.shape
    nc = S // L
    kern = functools.partial(_mlstm_kernel, L=L)
    cur = lambda b, c: (b, jnp.minimum(c, nc - 1), 0)
    prev = lambda b, c: (b, jnp.maximum(c - 1, 0), 0)
    row_spec = lambda n, im: pl.BlockSpec((1, L, n), im)
    return pl.pallas_call(
        kern,
        grid=(B, nc + 1),
        in_specs=[
            row_spec(ML_W, cur), row_spec(ML_W, cur), row_spec(ML_W, cur), row_spec(ML_W, prev),
            row_spec(LANES, lambda b, c: (b, 0, 0)),
            pl.BlockSpec((1, _N_GT, L), lambda b, c: (b, 0, 0)),
            row_spec(LANES, lambda b, c: (b, jnp.minimum(c + 1, nc - 1), 0)),
            pl.BlockSpec((1, _N_GT, L), lambda b, c: (b, 0, jnp.minimum(c + 1, nc - 1))),
            _const_spec((1, ML_W)),
        ],
        out_specs=row_spec(ML_W, prev),
        out_shape=jax.ShapeDtypeStruct((B, S, ML_W), BF16),
        scratch_shapes=[
            pltpu.VMEM((ML_HEADS, 2 * ML_DH, ML_DH), F32),
            pltpu.VMEM((8, LANES), F32),
            pltpu.VMEM((2, ML_HEADS, L, L), F32),
            pltpu.VMEM((2, ML_HEADS, 4, L), F32),
            pltpu.VMEM((2, ML_HEADS, ML_DH, L), F32),
        ],
        compiler_params=pltpu.CompilerParams(
            dimension_semantics=("arbitrary", "arbitrary"), vmem_limit_bytes=VMEM_LIMIT),
        name="mlstm",
    )(q, k, v, o, gate, gatet, gate, gatet, nrm)


_HP = 4
_VROWS = FX_DH + 16


def _fox_kernel(q_ref, k_ref, vt_ref, o_ref, m_ref, acc_ref, *, tq, tkc):
    qi = pl.program_id(2)
    nsub = tkc // tq
    n_full = qi // nsub
    rem = qi % nsub
    key_i = lax.broadcasted_iota(jnp.int32, (tq, tq), 0)
    qry_i = lax.broadcasted_iota(jnp.int32, (tq, tq), 1)

    m_ref[...] = jnp.full_like(m_ref, NEG_BIG)
    acc_ref[...] = jnp.zeros_like(acc_ref)
    ones_rows = (lax.broadcasted_iota(jnp.int32, (_VROWS - FX_DH, tq), 0) == 0).astype(BF16)

    def chunk(j, n_plain, diagonal):
        state = [(m_ref[hd], acc_ref[hd]) for hd in range(_HP)]
        n_sub = n_plain + int(diagonal)
        scores = {}
        for i in range(n_sub):
            ks = slice(i * tq, (i + 1) * tq)
            for hd in range(_HP):
                st = lax.dot_general(k_ref[0, hd, j, ks, :], q_ref[0, hd],
                                     (((1,), (1,)), ((), ())),
                                     preferred_element_type=F32)
                if diagonal and i == n_plain:
                    st = jnp.where(qry_i >= key_i, st, NEG_BIG)
                scores[i, hd] = st
        for i in range(n_sub):
            ks = slice(i * tq, (i + 1) * tq)
            for hd in range(_HP):
                m, acc = state[hd]
                st = scores[i, hd]
                m_new = jnp.maximum(m, jnp.max(st, axis=0, keepdims=True))
                alpha = jnp.exp2(m - m_new)
                p = jnp.exp2(st - m_new).astype(BF16)
                vt = jnp.concatenate(
                    [vt_ref[0, j, hd * FX_DH:(hd + 1) * FX_DH, ks], ones_rows], axis=0)
                acc = alpha * acc + jnp.dot(vt, p, preferred_element_type=F32)
                state[hd] = (m_new, acc)
        for hd in range(_HP):
            m_ref[hd], acc_ref[hd] = state[hd]

    def body(j, _):
        chunk(j, nsub, False)
        return 0

    lax.fori_loop(0, n_full, body, 0)
    for r in range(nsub):
        pl.when(rem == r)(functools.partial(chunk, n_full, r, True))
    out_t = jnp.concatenate(
        [acc_ref[hd, 0:FX_DH, :] / acc_ref[hd, FX_DH:FX_DH + 1, :] for hd in range(_HP)],
        axis=0)
    o_ref[0] = out_t.T.astype(BF16)


def _fox(fq, fk, fvt, *, tq, tkc):
    B, H, S, _ = fq.shape
    n = S // tkc
    fk5 = fk.reshape(B, H, n, tkc, LANES)
    kern = functools.partial(_fox_kernel, tq=tq, tkc=tkc)
    return pl.pallas_call(
        kern,
        grid=(B, H // _HP, S // tq),
        in_specs=[
            pl.BlockSpec((1, _HP, tq, LANES), lambda b, h, q: (b, h, q, 0)),
            pl.BlockSpec((1, _HP, n, tkc, LANES), lambda b, h, q: (b, h, 0, 0, 0)),
            pl.BlockSpec((1, n, _HP * FX_DH, tkc), lambda b, h, q: (b, 0, h, 0)),
        ],
        out_specs=pl.BlockSpec((1, tq, _HP * FX_DH), lambda b, h, q: (b, q, h)),
        out_shape=jax.ShapeDtypeStruct((B, S, FX_W), BF16),
        scratch_shapes=[
            pltpu.VMEM((_HP, 1, tq), F32),
            pltpu.VMEM((_HP, _VROWS, tq), F32),
        ],
        compiler_params=pltpu.CompilerParams(
            dimension_semantics=("arbitrary", "arbitrary", "arbitrary"),
            vmem_limit_bytes=VMEM_LIMIT),
        name="fox",
    )(fq, fk5, fvt)


def _memkv_kernel(mem_ref, g_ref, w_ref, k_ref, v_ref):
    mn = _rms(mem_ref[0], g_ref[...]).astype(BF16)
    k_ref[0] = jnp.dot(mn, w_ref[:, 0:D_MODEL], preferred_element_type=F32).astype(BF16)
    v_ref[0] = jnp.dot(mn, w_ref[:, D_MODEL:2 * D_MODEL], preferred_element_type=F32).astype(BF16)


def _memkv(mem, ln_mem, w_xkv):
    B, M, _ = mem.shape
    spec = pl.BlockSpec((1, M, D_MODEL), lambda b: (b, 0, 0))
    return pl.pallas_call(
        _memkv_kernel,
        grid=(B,),
        in_specs=[spec, _const_spec((1, D_MODEL)), _const_spec(w_xkv.shape)],
        out_specs=(spec, spec),
        out_shape=(jax.ShapeDtypeStruct((B, M, D_MODEL), BF16),) * 2,
        compiler_params=pltpu.CompilerParams(
            dimension_semantics=("arbitrary",), vmem_limit_bytes=VMEM_LIMIT),
        name="memkv",
    )(mem, ln_mem, w_xkv)


def _xattn_kernel(x_ref, ml_ref, fx_ref, wout_ref, lnx_ref, wq_ref, km_ref, vm_ref, wo_ref,
                  out_ref):
    x1 = (x_ref[0]
          + jnp.dot(ml_ref[0], wout_ref[0:ML_W, :], preferred_element_type=F32)
          + jnp.dot(fx_ref[0], wout_ref[ML_W:ML_W + FX_W, :], preferred_element_type=F32))
    hx = _rms(x1, lnx_ref[...]).astype(BF16)
    q = (jnp.dot(hx, wq_ref[...], preferred_element_type=F32) * (X_DH ** -0.5)).astype(BF16)
    km = km_ref[0]
    vm = vm_ref[0]
    heads = []
    for hd in range(X_HEADS):
        sl = slice(hd * X_DH, (hd + 1) * X_DH)
        s = lax.dot_general(q[:, sl], km[:, sl], (((1,), (1,)), ((), ())),
                            preferred_element_type=F32)
        p = jnp.exp(s - jnp.max(s, axis=-1, keepdims=True))
        l = jnp.sum(p, axis=-1, keepdims=True)
        oh = jnp.dot(p.astype(BF16), vm[:, sl], preferred_element_type=F32) / l
        heads.append(oh.astype(BF16))
    o = jnp.concatenate(heads, axis=1)
    out_ref[0] = x1 + jnp.dot(o, wo_ref[...], preferred_element_type=F32)


def _xattn(x, ml, fx, w_out, ln_x, w_xq, kmem, vmem, w_xo, *, tm):
    B, S, _ = x.shape
    M = kmem.shape[1]
    row_spec = lambda n: pl.BlockSpec((1, tm, n), lambda b, s: (b, s, 0))
    mem_spec = pl.BlockSpec((1, M, D_MODEL), lambda b, s: (b, 0, 0))
    return pl.pallas_call(
        _xattn_kernel,
        grid=(B, S // tm),
        in_specs=[
            row_spec(D_MODEL), row_spec(ML_W), row_spec(FX_W),
            _const_spec(w_out.shape), _const_spec((1, D_MODEL)), _const_spec(w_xq.shape),
            mem_spec, mem_spec, _const_spec(w_xo.shape),
        ],
        out_specs=row_spec(D_MODEL),
        out_shape=jax.ShapeDtypeStruct((B, S, D_MODEL), F32),
        compiler_params=pltpu.CompilerParams(
            dimension_semantics=("arbitrary", "arbitrary"), vmem_limit_bytes=VMEM_LIMIT),
        name="xattn",
    )(x, ml, fx, w_out, ln_x, w_xq, kmem, vmem, w_xo)


_FF_CHUNK = 1024


def _mlp_kernel(x_ref, ln2_ref, w1_ref, w2_ref, lnf_ref, out_ref):
    x = x_ref[0]
    h = _rms(x, ln2_ref[...]).astype(BF16)
    acc = x
    for c in range(D_FF // _FF_CHUNK):
        sl = slice(c * _FF_CHUNK, (c + 1) * _FF_CHUNK)
        u = jnp.maximum(jnp.dot(h, w1_ref[:, sl], preferred_element_type=F32), 0.0)
        acc = acc + jnp.dot((u * u).astype(BF16), w2_ref[sl, :], preferred_element_type=F32)
    out_ref[0] = _rms(acc, lnf_ref[...])


def _mlp(x, ln2, w1, w2, ln_f, *, tm):
    B, S, _ = x.shape
    row_spec = pl.BlockSpec((1, tm, D_MODEL), lambda b, s: (b, s, 0))
    return pl.pallas_call(
        _mlp_kernel,
        grid=(B, S // tm),
        in_specs=[row_spec, _const_spec((1, D_MODEL)), _const_spec(w1.shape),
                  _const_spec(w2.shape), _const_spec((1, D_MODEL))],
        out_specs=row_spec,
        out_shape=jax.ShapeDtypeStruct((B, S, D_MODEL), F32),
        compiler_params=pltpu.CompilerParams(
            dimension_semantics=("arbitrary", "arbitrary"), vmem_limit_bytes=VMEM_LIMIT),
        name="mlp",
    )(x, ln2, w1, w2, ln_f)


def _layer(x, mem, ln1, w_in, conv_w, conv_b, b_i, b_f, ml_norm, fx_b_f, w_out,
           ln_x, ln_mem, w_xq, w_xkv, w_xo, ln2, w_ff1, w_ff2, ln_f_or_none):
    B, S, _ = x.shape
    t = _tiles(S)
    o_qk, o_v, o_o = 0, 2 * ML_W, 3 * ML_W
    o_i = 4 * ML_W
    o_f = o_i + ML_HEADS
    o_fq = o_f + ML_HEADS
    o_fk, o_fv = o_fq + FX_W, o_fq + 2 * FX_W
    o_ff = o_fq + 3 * FX_W
    w_gate = jnp.concatenate(
        [w_in[:, o_i:o_fq], w_in[:, o_ff:o_ff + FX_HEADS]], axis=1)
    w_main = jnp.concatenate(
        [w_in[:, o_qk:o_i], w_in[:, o_fq:o_fv],
         jnp.pad(w_gate, ((0, 0), (0, LANES - _N_GT)))], axis=1).astype(BF16)
    w_t = jnp.concatenate([w_in[:, o_fv:o_ff], w_gate], axis=1).T.astype(BF16)
    bias = jnp.concatenate([b_i, b_f, fx_b_f]).astype(F32)
    brow = jnp.pad(bias, (0, LANES - _N_GT)).reshape(1, LANES)
    bcol = bias.reshape(_N_GT, 1)

    mq, mk, v, o, gate, gatet, fq, fk, fvt = _proj(
        x, ln1.reshape(1, -1), w_main, w_t, brow, bcol, conv_w, conv_b.reshape(1, -1),
        tm=t["tm"], tkc=t["tkc"], L=t["chunk"])
    ml = _mlstm(mq, mk, v, o, gate, gatet, ml_norm.reshape(1, -1), L=t["chunk"])
    fx = _fox(fq, fk, fvt, tq=t["tq"], tkc=t["tkc"])
    kmem, vmem = _memkv(mem, ln_mem.reshape(1, -1), w_xkv.astype(BF16))
    x2 = _xattn(x, ml, fx, w_out.astype(BF16), ln_x.reshape(1, -1), w_xq.astype(BF16),
                kmem, vmem, w_xo.astype(BF16), tm=t["tail"])
    return _mlp(x2, ln2.reshape(1, -1), w_ff1.astype(BF16), w_ff2.astype(BF16),
                ln_f_or_none.reshape(1, -1), tm=t["tail"])


def kernel(x, mem, ln1, w_in, ml_conv_w, ml_conv_b, ml_b_i, ml_b_f, ml_norm, fx_b_f, w_out,
           ln_x, ln_mem, w_xq, w_xkv, w_xo, ln2, w_ff1, w_ff2, ln_f):
    depth = w_in.shape[0]
    assert depth == 1, "single-layer problem: the final norm is fused into the layer's MLP kernel"
    return _layer(x, mem, ln1[0], w_in[0], ml_conv_w[0], ml_conv_b[0], ml_b_i[0], ml_b_f[0],
                  ml_norm[0], fx_b_f[0], w_out[0], ln_x[0], ln_mem[0], w_xq[0], w_xkv[0],
                  w_xo[0], ln2[0], w_ff1[0], w_ff2[0], ln_f)
```

```python
import functools

import jax
import jax.numpy as jnp
from jax import lax
from jax.experimental import pallas as pl
from jax.experimental.pallas import tpu as pltpu

F32 = jnp.float32
BF16 = jnp.bfloat16

D_MODEL = 1024
ML_HEADS = 4
ML_DH = 128
ML_W = ML_HEADS * ML_DH
CONV_W = 4
FX_HEADS = 8
FX_DH = 64
FX_W = FX_HEADS * FX_DH
X_HEADS = 4
X_DH = D_MODEL // X_HEADS
D_FF = 4 * D_MODEL
EPS = 1e-6

LANES = 128
NEG_BIG = -1e30
LOG2E = 1.4426950408889634
_TAIL = 8
VMEM_LIMIT = 56 * 1024 * 1024

_C_QK = 0
_C_V = _C_QK + 2 * ML_W
_C_O = _C_V + ML_W
_C_FQ = _C_O + ML_W
_C_FK = _C_FQ + FX_W
_C_G = _C_FK + FX_W
_C_END = _C_G + LANES
_R_GT = FX_W
_N_GT = 16

_AUG0 = FX_DH


def _tiles(S):
    tm = min(512, S)
    return dict(
        tm=tm,
        chunk=min(256, S),
        tq=min(256, S),
        tkc=max(tm, min(2048, S)),
        tail=min(512, S),
    )


def _rms(x, g):
    return x * lax.rsqrt(jnp.mean(x * x, axis=-1, keepdims=True) + EPS) * g


def _split3(x):
    hi = x.astype(BF16)
    r = x - hi.astype(F32)
    mid = r.astype(BF16)
    lo = (r - mid.astype(F32)).astype(BF16)
    return hi, mid, lo


def _log_sigmoid(x):
    return jnp.minimum(x, 0.0) - jnp.log1p(jnp.exp(-jnp.abs(x)))


def _const_spec(shape):
    nd = len(shape)
    return pl.BlockSpec(shape, lambda *_: (0,) * nd, pipeline_mode=pl.Buffered(1))


def _proj_kernel(x_ref, g_ref, w_ref, wt_ref, brow_ref, bcol_ref, cw_ref, cb_ref,
                 mq_ref, mk_ref, v_ref, o_ref, gate_ref, gatet_ref, fq_ref, fk_ref, fvt_ref,
                 carry_ref, zbuf_ref, *, tm, L):
    @pl.when(pl.program_id(1) == 0)
    def _():
        carry_ref[...] = jnp.zeros_like(carry_ref)
        zbuf_ref[0:_TAIL, :] = jnp.zeros((_TAIL, 2 * ML_W), F32)

    x = x_ref[0]
    h = _rms(x, g_ref[...]).astype(BF16)

    def proj(c0, c1):
        return jnp.dot(h, w_ref[:, c0:c1], preferred_element_type=F32)

    lane = lax.broadcasted_iota(jnp.int32, (tm, LANES), 1)

    def in_lanes(lo, n):
        return (lane >= lo) & (lane < lo + n)

    gpre = proj(_C_G, _C_END) + brow_ref[...]
    zq = proj(_C_FQ, _C_FK) * (FX_DH ** -0.5 * LOG2E)
    zk = proj(_C_FK, _C_G)
    gls = _log_sigmoid(gpre)
    tril = (lax.broadcasted_iota(jnp.int32, (L, L), 1)
            <= lax.broadcasted_iota(jnp.int32, (L, L), 0)).astype(BF16)
    is_ig = lax.broadcasted_iota(jnp.int32, (L, LANES), 1) < ML_HEADS
    parts = _split3(gls)
    carry = carry_ref[...]
    csum_chunks = []
    for ci in range(tm // L):
        rows = slice(ci * L, (ci + 1) * L)
        local = sum(jnp.dot(tril, part[rows], preferred_element_type=F32) for part in parts)
        gate_ref[0, rows, :] = LOG2E * jnp.where(is_ig, gpre[rows], local)
        csum_chunks.append(carry + local)
        carry = carry + local[L - 1:L, :]
    carry_ref[...] = carry
    csum = jnp.concatenate(csum_chunks, axis=0)

    c_hi, c_mid, c_lo = (p.astype(F32) for p in _split3(csum * LOG2E))
    src = 2 * ML_HEADS
    cq = jnp.where(in_lanes(_AUG0, 8), pltpu.roll(c_hi, _AUG0 - src, 1),
         jnp.where(in_lanes(_AUG0 + 8, 8), pltpu.roll(c_mid, _AUG0 + 8 - src, 1),
         jnp.where(in_lanes(_AUG0 + 16, 8), pltpu.roll(c_lo, _AUG0 + 16 - src, 1), 0.0)))
    ck = -pltpu.roll(cq, 24, 1)
    lane_row = lax.broadcasted_iota(jnp.int32, (1, LANES), 1)
    for hd in range(FX_HEADS):
        grp = hd // 2
        zq_h = zq[:, grp * LANES:(grp + 1) * LANES]
        zk_h = zk[:, grp * LANES:(grp + 1) * LANES]
        if hd % 2:
            zq_h = pltpu.roll(zq_h, LANES // 2, 1)
            zk_h = pltpu.roll(zk_h, LANES // 2, 1)
        hot = [(lane_row == base0 + 8 * j + hd) for j in range(3) for base0 in (_AUG0, _AUG0 + 24)]
        hot_k = (hot[0] | hot[2] | hot[4]).astype(F32)
        hot_q = (hot[1] | hot[3] | hot[5]).astype(F32)
        fq_ref[0, hd] = jnp.where(lane < _AUG0, zq_h, cq + hot_q).astype(BF16)
        fk_ref[0, hd] = jnp.where(lane < _AUG0, zk_h, ck + hot_k).astype(BF16)

    zqk = proj(_C_QK, _C_V)
    zbuf_ref[_TAIL:_TAIL + tm, :] = zqk
    conv = cb_ref[...] + zqk * cw_ref[CONV_W - 1:CONV_W, :]
    for j in range(CONV_W - 1):
        off = _TAIL - (CONV_W - 1) + j
        conv = conv + zbuf_ref[off:off + tm, :] * cw_ref[j:j + 1, :]
    zbuf_ref[0:_TAIL, :] = zbuf_ref[tm:tm + _TAIL, :]
    act = conv * jax.nn.sigmoid(conv)
    mq_ref[0] = act[:, 0:ML_W].astype(BF16)
    mk_ref[0] = (act[:, ML_W:2 * ML_W] * (ML_DH ** -0.5)).astype(BF16)

    zt = lax.dot_general(wt_ref[...], h, (((1,), (1,)), ((), ())), preferred_element_type=F32)
    fvt_ref[0, 0] = zt[0:FX_W, :].astype(BF16)
    gt = zt[_R_GT:_R_GT + _N_GT, :] + bcol_ref[...]
    glt = _log_sigmoid(gt)
    row = lax.broadcasted_iota(jnp.int32, (_N_GT, L), 0)
    triu = (lax.broadcasted_iota(jnp.int32, (L, L), 0)
            <= lax.broadcasted_iota(jnp.int32, (L, L), 1)).astype(BF16)
    for ci in range(tm // L):
        cols = slice(ci * L, (ci + 1) * L)
        brow = sum(jnp.dot(part[:, cols], triu, preferred_element_type=F32)
                   for part in _split3(glt))
        gatet_ref[0, :, cols] = LOG2E * jnp.where(row < ML_HEADS, gt[:, cols], brow)

    v_ref[0] = proj(_C_V, _C_O).astype(BF16)
    o_ref[0] = proj(_C_O, _C_FQ).astype(BF16)


def _proj(x, ln1, w_main, w_t, brow, bcol, conv_w, conv_b, *, tm, tkc, L):
    B, S, _ = x.shape
    ns = S // tm
    r = tkc // tm
    kern = functools.partial(_proj_kernel, tm=tm, L=L)
    row_spec = lambda n: pl.BlockSpec((1, tm, n), lambda b, s: (b, s, 0))
    out_shape = (
        jax.ShapeDtypeStruct((B, S, ML_W), BF16),
        jax.ShapeDtypeStruct((B, S, ML_W), BF16),
        jax.ShapeDtypeStruct((B, S, ML_W), BF16),
        jax.ShapeDtypeStruct((B, S, ML_W), BF16),
        jax.ShapeDtypeStruct((B, S, LANES), F32),
        jax.ShapeDtypeStruct((B, _N_GT, S), F32),
        jax.ShapeDtypeStruct((B, FX_HEADS, S, LANES), BF16),
        jax.ShapeDtypeStruct((B, FX_HEADS, S, LANES), BF16),
        jax.ShapeDtypeStruct((B, S // tkc, FX_W, tkc), BF16),
    )
    out_specs = (
        row_spec(ML_W), row_spec(ML_W), row_spec(ML_W), row_spec(ML_W), row_spec(LANES),
        pl.BlockSpec((1, _N_GT, tm), lambda b, s: (b, 0, s)),
        pl.BlockSpec((1, FX_HEADS, tm, LANES), lambda b, s: (b, 0, s, 0)),
        pl.BlockSpec((1, FX_HEADS, tm, LANES), lambda b, s: (b, 0, s, 0)),
        pl.BlockSpec((1, 1, FX_W, tm), lambda b, s: (b, s // r, 0, s % r)),
    )
    return pl.pallas_call(
        kern,
        grid=(B, ns),
        in_specs=[
            row_spec(D_MODEL),
            _const_spec((1, D_MODEL)),
            _const_spec(w_main.shape),
            _const_spec(w_t.shape),
            _const_spec((1, LANES)),
            _const_spec((_N_GT, 1)),
            _const_spec((CONV_W, 2 * ML_W)),
            _const_spec((1, 2 * ML_W)),
        ],
        out_specs=out_specs,
        out_shape=out_shape,
        scratch_shapes=[pltpu.VMEM((1, LANES), F32),
                        pltpu.VMEM((tm + _TAIL, 2 * ML_W), F32)],
        compiler_params=pltpu.CompilerParams(
            dimension_semantics=("arbitrary", "arbitrary"), vmem_limit_bytes=VMEM_LIMIT),
        name="proj",
    )(x, ln1, w_main, w_t, brow, bcol, conv_w, conv_b)


_ML_SL = [slice(hd * ML_DH, (hd + 1) * ML_DH) for hd in range(ML_HEADS)]
_V_INTER, _V_EM, _V_WA, _V_DECAY = 0, 1, 2, 3


def _mlstm_gates(gate, gatet, m_ref, et_ref, vec_ref, slot, *, L):
    src_i = lax.broadcasted_iota(jnp.int32, (L, L), 0)
    tgt_i = lax.broadcasted_iota(jnp.int32, (L, L), 1)
    causal_t = src_i <= tgt_i
    for hd in range(ML_HEADS):
        ig_row = gatet[hd:hd + 1, :]
        b_row = gatet[ML_HEADS + hd:ML_HEADS + hd + 1, :]
        r_col = gate[:, hd:hd + 1] - gate[:, ML_HEADS + hd:ML_HEADS + hd + 1]
        g = b_row[:, L - 1:L]
        m_prev = m_ref[hd:hd + 1, 0:1]
        dmat_t = jnp.where(causal_t, b_row + r_col, NEG_BIG)
        m_inter = b_row + m_prev
        m_t = jnp.maximum(m_inter, jnp.max(dmat_t, axis=0, keepdims=True))
        a_row = g - b_row + ig_row
        m_new = jnp.maximum(g + m_prev, jnp.max(a_row, axis=1, keepdims=True))
        m_ref[hd:hd + 1, :] = jnp.broadcast_to(m_new, (1, LANES))
        et_ref[slot, hd] = jnp.exp2(dmat_t - m_t)
        vec_ref[slot, hd, _V_INTER:_V_INTER + 1, :] = jnp.exp2(m_inter - m_t)
        vec_ref[slot, hd, _V_EM:_V_EM + 1, :] = jnp.exp2(-m_t)
        vec_ref[slot, hd, _V_WA:_V_WA + 1, :] = jnp.exp2(a_row - m_new)
        vec_ref[slot, hd, _V_DECAY:_V_DECAY + 1, :] = jnp.broadcast_to(
            jnp.exp2(g + m_prev - m_new), (1, L))


def _mlstm_mix(q_ref, k_ref, v_ref, et_ref, vec_ref, state_ref, hvt_ref, slot, *, L):
    one_rows = (lax.broadcasted_iota(jnp.int32, (ML_DH, L), 0) == 0).astype(BF16)
    heads = range(ML_HEADS)
    vec = [vec_ref[slot, hd] for hd in heads]
    q = [q_ref[0, :, _ML_SL[hd]] for hd in heads]
    k = [k_ref[0, :, _ML_SL[hd]] for hd in heads]
    vaug_t = [jnp.concatenate([v_ref[0, :, _ML_SL[hd]].T, one_rows], axis=0)
              for hd in heads]
    vaug_tw = [(vaug_t[hd].astype(F32) * vec[hd][_V_WA:_V_WA + 1, :]).astype(BF16) for hd in heads]
    s_t = [lax.dot_general(k[hd], q[hd], (((1,), (1,)), ((), ())),
                           preferred_element_type=F32) for hd in heads]
    qs_t = [lax.dot_general(state_ref[hd].astype(BF16), q[hd], (((1,), (1,)), ((), ())),
                            preferred_element_type=F32) for hd in heads]
    upd = [jnp.dot(vaug_tw[hd], k[hd], preferred_element_type=F32) for hd in heads]
    for hd in heads:
        state_ref[hd] = vec[hd][_V_DECAY:_V_DECAY + 1, 0:1] * state_ref[hd] + upd[hd]
    scores_t = [(s_t[hd] * et_ref[slot, hd]).astype(BF16) for hd in heads]
    nd_t = [jnp.dot(vaug_t[hd], scores_t[hd], preferred_element_type=F32)
            + vec[hd][_V_INTER:_V_INTER + 1, :] * qs_t[hd] for hd in heads]
    for hd in heads:
        den = nd_t[hd][ML_DH:ML_DH + 1, :]
        hvt_ref[slot, hd] = nd_t[hd][0:ML_DH, :] / jnp.maximum(
            jnp.abs(den), vec[hd][_V_EM:_V_EM + 1, :])


def _mlstm_out(hvt_ref, o_ref, nrm_ref, out_ref, slot):
    for hd in range(ML_HEADS):
        hv_t = hvt_ref[slot, hd]
        hn_t = hv_t * lax.rsqrt(jnp.mean(hv_t * hv_t, axis=0, keepdims=True) + EPS)
        og = jax.nn.sigmoid(o_ref[0, :, _ML_SL[hd]].astype(F32))
        out_ref[0, :, _ML_SL[hd]] = (hn_t.T * nrm_ref[:, _ML_SL[hd]] * og).astype(BF16)


def _mlstm_kernel(q_ref, k_ref, v_ref, o_ref, gate0_ref, gatet0_ref, gate_ref, gatet_ref, nrm_ref,
                  out_ref, state_ref, m_ref, e_ref, vec_ref, hv_ref, *, L):
    c = pl.program_id(1)

    @pl.when(c == 0)
    def _():
        state_ref[...] = jnp.zeros_like(state_ref)
        m_ref[...] = jnp.zeros_like(m_ref)
        hv_ref[...] = jnp.zeros_like(hv_ref)
        _mlstm_gates(gate0_ref[0], gatet0_ref[0], m_ref, e_ref, vec_ref, 0, L=L)

    slot = c % 2
    _mlstm_out(hv_ref, o_ref, nrm_ref, out_ref, 1 - slot)
    _mlstm_mix(q_ref, k_ref, v_ref, e_ref, vec_ref, state_ref, hv_ref, slot, L=L)
    _mlstm_gates(gate_ref[0], gatet_ref[0], m_ref, e_ref, vec_ref, 1 - slot, L=L)


def _mlstm(q, k, v, o, gate, gatet, nrm, *, L):
    B, S, _ = q.shape
    nc = S // L
    kern = functools.partial(_mlstm_kernel, L=L)
    cur = lambda b, c: (b, jnp.minimum(c, nc - 1), 0)
    prev = lambda b, c: (b, jnp.maximum(c - 1, 0), 0)
    row_spec = lambda n, im: pl.BlockSpec((1, L, n), im)
    return pl.pallas_call(
        kern,
        grid=(B, nc + 1),
        in_specs=[
            row_spec(ML_W, cur), row_spec(ML_W, cur), row_spec(ML_W, cur), row_spec(ML_W, prev),
            row_spec(LANES, lambda b, c: (b, 0, 0)),
            pl.BlockSpec((1, _N_GT, L), lambda b, c: (b, 0, 0)),
            row_spec(LANES, lambda b, c: (b, jnp.minimum(c + 1, nc - 1), 0)),
            pl.BlockSpec((1, _N_GT, L), lambda b, c: (b, 0, jnp.minimum(c + 1, nc - 1))),
            _const_spec((1, ML_W)),
        ],
        out_specs=row_spec(ML_W, prev),
        out_shape=jax.ShapeDtypeStruct((B, S, ML_W), BF16),
        scratch_shapes=[
            pltpu.VMEM((ML_HEADS, 2 * ML_DH, ML_DH), F32),
            pltpu.VMEM((8, LANES), F32),
            pltpu.VMEM((2, ML_HEADS, L, L), F32),
            pltpu.VMEM((2, ML_HEADS, 4, L), F32),
            pltpu.VMEM((2, ML_HEADS, ML_DH, L), F32),
        ],
        compiler_params=pltpu.CompilerParams(
            dimension_semantics=("arbitrary", "arbitrary"), vmem_limit_bytes=VMEM_LIMIT),
        name="mlstm",
    )(q, k, v, o, gate, gatet, gate, gatet, nrm)


_HP = 4
_VROWS = FX_DH + 16
_LAG = 3 * _HP


def _fox_kernel(q_ref, k_ref, vt_ref, o_ref, m_ref, acc_ref, *, tq, tkc):
    qi = pl.program_id(2)
    nsub = tkc // tq
    n_full = qi // nsub
    rem = qi % nsub
    key_i = lax.broadcasted_iota(jnp.int32, (tq, tq), 0)
    qry_i = lax.broadcasted_iota(jnp.int32, (tq, tq), 1)

    m_ref[...] = jnp.full_like(m_ref, NEG_BIG)
    acc_ref[...] = jnp.zeros_like(acc_ref)
    ones_rows = (lax.broadcasted_iota(jnp.int32, (_VROWS - FX_DH, tq), 0) == 0).astype(BF16)

    def chunk(j, n_plain, diagonal):
        state = [(m_ref[hd], acc_ref[hd]) for hd in range(_HP)]
        tiles = [(i, hd) for i in range(n_plain + int(diagonal)) for hd in range(_HP)]
        scores = {}

        def qk(i, hd):
            st = lax.dot_general(k_ref[0, hd, j, i * tq:(i + 1) * tq, :], q_ref[0, hd],
                                 (((1,), (1,)), ((), ())),
                                 preferred_element_type=F32)
            if diagonal and i == n_plain:
                st = jnp.where(qry_i >= key_i, st, NEG_BIG)
            scores[i, hd] = st

        def softmax_pv(i, hd):
            m, acc = state[hd]
            st = scores.pop((i, hd))
            m_new = jnp.maximum(m, jnp.max(st, axis=0, keepdims=True))
            alpha = jnp.exp2(m - m_new)
            p = jnp.exp2(st - m_new).astype(BF16)
            vt = jnp.concatenate(
                [vt_ref[0, j, hd * FX_DH:(hd + 1) * FX_DH, i * tq:(i + 1) * tq], ones_rows],
                axis=0)
            acc = alpha * acc + jnp.dot(vt, p, preferred_element_type=F32)
            state[hd] = (m_new, acc)

        for t in range(len(tiles) + _LAG):
            if t < len(tiles):
                qk(*tiles[t])
            if t >= _LAG:
                softmax_pv(*tiles[t - _LAG])
        for hd in range(_HP):
            m_ref[hd], acc_ref[hd] = state[hd]

    def body(j, _):
        chunk(j, nsub, False)
        return 0

    lax.fori_loop(0, n_full, body, 0)
    for r in range(nsub):
        pl.when(rem == r)(functools.partial(chunk, n_full, r, True))
    out_t = jnp.concatenate(
        [acc_ref[hd, 0:FX_DH, :] / acc_ref[hd, FX_DH:FX_DH + 1, :] for hd in range(_HP)],
        axis=0)
    o_ref[0] = out_t.T.astype(BF16)


def _fox(fq, fk, fvt, *, tq, tkc):
    B, H, S, _ = fq.shape
    n = S // tkc
    fk5 = fk.reshape(B, H, n, tkc, LANES)
    kern = functools.partial(_fox_kernel, tq=tq, tkc=tkc)
    return pl.pallas_call(
        kern,
        grid=(B, H // _HP, S // tq),
        in_specs=[
            pl.BlockSpec((1, _HP, tq, LANES), lambda b, h, q: (b, h, q, 0)),
            pl.BlockSpec((1, _HP, n, tkc, LANES), lambda b, h, q: (b, h, 0, 0, 0)),
            pl.BlockSpec((1, n, _HP * FX_DH, tkc), lambda b, h, q: (b, 0, h, 0)),
        ],
        out_specs=pl.BlockSpec((1, tq, _HP * FX_DH), lambda b, h, q: (b, q, h)),
        out_shape=jax.ShapeDtypeStruct((B, S, FX_W), BF16),
        scratch_shapes=[
            pltpu.VMEM((_HP, 1, tq), F32),
            pltpu.VMEM((_HP, _VROWS, tq), F32),
        ],
        compiler_params=pltpu.CompilerParams(
            dimension_semantics=("arbitrary", "arbitrary", "arbitrary"),
            vmem_limit_bytes=VMEM_LIMIT),
        name="fox",
    )(fq, fk5, fvt)


def _memkv_kernel(mem_ref, g_ref, w_ref, k_ref, v_ref):
    mn = _rms(mem_ref[0], g_ref[...]).astype(BF16)
    k_ref[0] = jnp.dot(mn, w_ref[:, 0:D_MODEL], preferred_element_type=F32).astype(BF16)
    v_ref[0] = jnp.dot(mn, w_ref[:, D_MODEL:2 * D_MODEL], preferred_element_type=F32).astype(BF16)


def _memkv(mem, ln_mem, w_xkv):
    B, M, _ = mem.shape
    spec = pl.BlockSpec((1, M, D_MODEL), lambda b: (b, 0, 0))
    return pl.pallas_call(
        _memkv_kernel,
        grid=(B,),
        in_specs=[spec, _const_spec((1, D_MODEL)), _const_spec(w_xkv.shape)],
        out_specs=(spec, spec),
        out_shape=(jax.ShapeDtypeStruct((B, M, D_MODEL), BF16),) * 2,
        compiler_params=pltpu.CompilerParams(
            dimension_semantics=("arbitrary",), vmem_limit_bytes=VMEM_LIMIT),
        name="memkv",
    )(mem, ln_mem, w_xkv)


_ROW_GROUPS = 2


def _xattn_kernel(x_ref, ml_ref, fx_ref, wout_ref, lnx_ref, wq_ref, km_ref, vm_ref, wo_ref,
                  out_ref):
    tm = x_ref.shape[1]
    groups = [slice(r, r + tm // _ROW_GROUPS) for r in range(0, tm, tm // _ROW_GROUPS)]
    sls = [slice(hd * X_DH, (hd + 1) * X_DH) for hd in range(X_HEADS)]
    km = km_ref[0]
    vm = vm_ref[0]
    x1 = [x_ref[0, g, :]
          + jnp.dot(ml_ref[0, g, :], wout_ref[0:ML_W, :], preferred_element_type=F32)
          + jnp.dot(fx_ref[0, g, :], wout_ref[ML_W:ML_W + FX_W, :], preferred_element_type=F32)
          for g in groups]
    q = [(jnp.dot(_rms(x, lnx_ref[...]).astype(BF16), wq_ref[...], preferred_element_type=F32)
          * (X_DH ** -0.5)).astype(BF16) for x in x1]
    s = [[lax.dot_general(qg[:, sl], km[:, sl], (((1,), (1,)), ((), ())),
                          preferred_element_type=F32) for sl in sls] for qg in q]
    o = []
    for sg in s:
        heads = []
        for hd, sl in enumerate(sls):
            p = jnp.exp(sg[hd] - jnp.max(sg[hd], axis=-1, keepdims=True))
            l = jnp.sum(p, axis=-1, keepdims=True)
            oh = jnp.dot(p.astype(BF16), vm[:, sl], preferred_element_type=F32) / l
            heads.append(oh.astype(BF16))
        o.append(jnp.concatenate(heads, axis=1))
    for g, xg, og in zip(groups, x1, o):
        out_ref[0, g, :] = xg + jnp.dot(og, wo_ref[...], preferred_element_type=F32)


def _xattn(x, ml, fx, w_out, ln_x, w_xq, kmem, vmem, w_xo, *, tm):
    B, S, _ = x.shape
    M = kmem.shape[1]
    row_spec = lambda n: pl.BlockSpec((1, tm, n), lambda b, s: (b, s, 0))
    mem_spec = pl.BlockSpec((1, M, D_MODEL), lambda b, s: (b, 0, 0))
    return pl.pallas_call(
        _xattn_kernel,
        grid=(B, S // tm),
        in_specs=[
            row_spec(D_MODEL), row_spec(ML_W), row_spec(FX_W),
            _const_spec(w_out.shape), _const_spec((1, D_MODEL)), _const_spec(w_xq.shape),
            mem_spec, mem_spec, _const_spec(w_xo.shape),
        ],
        out_specs=row_spec(D_MODEL),
        out_shape=jax.ShapeDtypeStruct((B, S, D_MODEL), F32),
        compiler_params=pltpu.CompilerParams(
            dimension_semantics=("arbitrary", "arbitrary"), vmem_limit_bytes=VMEM_LIMIT),
        name="xattn",
    )(x, ml, fx, w_out, ln_x, w_xq, kmem, vmem, w_xo)


_FF_CHUNK = 1024


def _mlp_kernel(x_ref, ln2_ref, w1_ref, w2_ref, lnf_ref, out_ref):
    x = x_ref[0]
    h = _rms(x, ln2_ref[...]).astype(BF16)
    acc = x
    for c in range(D_FF // _FF_CHUNK):
        sl = slice(c * _FF_CHUNK, (c + 1) * _FF_CHUNK)
        u = jnp.maximum(jnp.dot(h, w1_ref[:, sl], preferred_element_type=F32), 0.0)
        acc = acc + jnp.dot((u * u).astype(BF16), w2_ref[sl, :], preferred_element_type=F32)
    out_ref[0] = _rms(acc, lnf_ref[...])


def _mlp(x, ln2, w1, w2, ln_f, *, tm):
    B, S, _ = x.shape
    row_spec = pl.BlockSpec((1, tm, D_MODEL), lambda b, s: (b, s, 0))
    return pl.pallas_call(
        _mlp_kernel,
        grid=(B, S // tm),
        in_specs=[row_spec, _const_spec((1, D_MODEL)), _const_spec(w1.shape),
                  _const_spec(w2.shape), _const_spec((1, D_MODEL))],
        out_specs=row_spec,
        out_shape=jax.ShapeDtypeStruct((B, S, D_MODEL), F32),
        compiler_params=pltpu.CompilerParams(
            dimension_semantics=("arbitrary", "arbitrary"), vmem_limit_bytes=VMEM_LIMIT),
        name="mlp",
    )(x, ln2, w1, w2, ln_f)


def _layer(x, mem, ln1, w_in, conv_w, conv_b, b_i, b_f, ml_norm, fx_b_f, w_out,
           ln_x, ln_mem, w_xq, w_xkv, w_xo, ln2, w_ff1, w_ff2, ln_f_or_none):
    B, S, _ = x.shape
    t = _tiles(S)
    o_qk, o_v, o_o = 0, 2 * ML_W, 3 * ML_W
    o_i = 4 * ML_W
    o_f = o_i + ML_HEADS
    o_fq = o_f + ML_HEADS
    o_fk, o_fv = o_fq + FX_W, o_fq + 2 * FX_W
    o_ff = o_fq + 3 * FX_W
    w_gate = jnp.concatenate(
        [w_in[:, o_i:o_fq], w_in[:, o_ff:o_ff + FX_HEADS]], axis=1)
    w_main = jnp.concatenate(
        [w_in[:, o_qk:o_i], w_in[:, o_fq:o_fv],
         jnp.pad(w_gate, ((0, 0), (0, LANES - _N_GT)))], axis=1).astype(BF16)
    w_t = jnp.concatenate([w_in[:, o_fv:o_ff], w_gate], axis=1).T.astype(BF16)
    bias = jnp.concatenate([b_i, b_f, fx_b_f]).astype(F32)
    brow = jnp.pad(bias, (0, LANES - _N_GT)).reshape(1, LANES)
    bcol = bias.reshape(_N_GT, 1)

    mq, mk, v, o, gate, gatet, fq, fk, fvt = _proj(
        x, ln1.reshape(1, -1), w_main, w_t, brow, bcol, conv_w, conv_b.reshape(1, -1),
        tm=t["tm"], tkc=t["tkc"], L=t["chunk"])
    ml = _mlstm(mq, mk, v, o, gate, gatet, ml_norm.reshape(1, -1), L=t["chunk"])
    fx = _fox(fq, fk, fvt, tq=t["tq"], tkc=t["tkc"])
    kmem, vmem = _memkv(mem, ln_mem.reshape(1, -1), w_xkv.astype(BF16))
    x2 = _xattn(x, ml, fx, w_out.astype(BF16), ln_x.reshape(1, -1), w_xq.astype(BF16),
                kmem, vmem, w_xo.astype(BF16), tm=t["tail"])
    return _mlp(x2, ln2.reshape(1, -1), w_ff1.astype(BF16), w_ff2.astype(BF16),
                ln_f_or_none.reshape(1, -1), tm=t["tail"])


def kernel(x, mem, ln1, w_in, ml_conv_w, ml_conv_b, ml_b_i, ml_b_f, ml_norm, fx_b_f, w_out,
           ln_x, ln_mem, w_xq, w_xkv, w_xo, ln2, w_ff1, w_ff2, ln_f):
    depth = w_in.shape[0]
    assert depth == 1, "single-layer problem: the final norm is fused into the layer's MLP kernel"
    return _layer(x, mem, ln1[0], w_in[0], ml_conv_w[0], ml_conv_b[0], ml_b_i[0], ml_b_f[0],
                  ml_norm[0], fx_b_f[0], w_out[0], ln_x[0], ln_mem[0], w_xq[0], w_xkv[0],
                  w_xo[0], ln2[0], w_ff1[0], w_ff2[0], ln_f)
```

```python
import functools

import jax
import jax.numpy as jnp
from jax import lax
from jax.experimental import pallas as pl
from jax.experimental.pallas import tpu as pltpu

F32 = jnp.float32
BF16 = jnp.bfloat16

D_MODEL = 1024
ML_HEADS = 4
ML_DH = 128
ML_W = ML_HEADS * ML_DH
CONV_W = 4
FX_HEADS = 8
FX_DH = 64
FX_W = FX_HEADS * FX_DH
X_HEADS = 4
X_DH = D_MODEL // X_HEADS
D_FF = 4 * D_MODEL
EPS = 1e-6

LANES = 128
NEG_BIG = -1e30
LOG2E = 1.4426950408889634
_TAIL = 8
VMEM_LIMIT = 56 * 1024 * 1024

_C_QK = 0
_C_V = _C_QK + 2 * ML_W
_C_O = _C_V + ML_W
_C_FQ = _C_O + ML_W
_C_FK = _C_FQ + FX_W
_C_G = _C_FK + FX_W
_C_END = _C_G + LANES
_R_GT = FX_W
_N_GT = 16

_AUG0 = FX_DH


def _tiles(S):
    tm = min(512, S)
    return dict(
        tm=tm,
        chunk=min(256, S),
        tq=min(256, S),
        tkc=max(tm, min(2048, S)),
        tail=min(512, S),
    )


def _rms(x, g):
    return x * lax.rsqrt(jnp.mean(x * x, axis=-1, keepdims=True) + EPS) * g


def _split3(x):
    hi = x.astype(BF16)
    r = x - hi.astype(F32)
    mid = r.astype(BF16)
    lo = (r - mid.astype(F32)).astype(BF16)
    return hi, mid, lo


def _log_sigmoid(x):
    return jnp.minimum(x, 0.0) - jnp.log1p(jnp.exp(-jnp.abs(x)))


def _const_spec(shape):
    nd = len(shape)
    return pl.BlockSpec(shape, lambda *_: (0,) * nd, pipeline_mode=pl.Buffered(1))


def _proj_kernel(x_ref, g_ref, w_ref, wt_ref, brow_ref, bcol_ref, cw_ref, cb_ref,
                 mq_ref, mk_ref, v_ref, o_ref, gate_ref, gatet_ref, fq_ref, fk_ref, fvt_ref,
                 carry_ref, zbuf_ref, *, tm, L):
    @pl.when(pl.program_id(1) == 0)
    def _():
        carry_ref[...] = jnp.zeros_like(carry_ref)
        zbuf_ref[0:_TAIL, :] = jnp.zeros((_TAIL, 2 * ML_W), F32)

    x = x_ref[0]
    h = _rms(x, g_ref[...]).astype(BF16)

    def proj(c0, c1):
        return jnp.dot(h, w_ref[:, c0:c1], preferred_element_type=F32)

    lane = lax.broadcasted_iota(jnp.int32, (tm, LANES), 1)

    def in_lanes(lo, n):
        return (lane >= lo) & (lane < lo + n)

    gpre = proj(_C_G, _C_END) + brow_ref[...]
    zq = proj(_C_FQ, _C_FK) * (FX_DH ** -0.5 * LOG2E)
    zk = proj(_C_FK, _C_G)
    gls = _log_sigmoid(gpre)
    tril = (lax.broadcasted_iota(jnp.int32, (L, L), 1)
            <= lax.broadcasted_iota(jnp.int32, (L, L), 0)).astype(BF16)
    is_ig = lax.broadcasted_iota(jnp.int32, (L, LANES), 1) < ML_HEADS
    parts = _split3(gls)
    carry = carry_ref[...]
    csum_chunks = []
    for ci in range(tm // L):
        rows = slice(ci * L, (ci + 1) * L)
        local = sum(jnp.dot(tril, part[rows], preferred_element_type=F32) for part in parts)
        gate_ref[0, rows, :] = LOG2E * jnp.where(is_ig, gpre[rows], local)
        csum_chunks.append(carry + local)
        carry = carry + local[L - 1:L, :]
    carry_ref[...] = carry
    csum = jnp.concatenate(csum_chunks, axis=0)

    c_hi, c_mid, c_lo = (p.astype(F32) for p in _split3(csum * LOG2E))
    src = 2 * ML_HEADS
    cq = jnp.where(in_lanes(_AUG0, 8), pltpu.roll(c_hi, _AUG0 - src, 1),
         jnp.where(in_lanes(_AUG0 + 8, 8), pltpu.roll(c_mid, _AUG0 + 8 - src, 1),
         jnp.where(in_lanes(_AUG0 + 16, 8), pltpu.roll(c_lo, _AUG0 + 16 - src, 1), 0.0)))
    ck = -pltpu.roll(cq, 24, 1)
    lane_row = lax.broadcasted_iota(jnp.int32, (1, LANES), 1)
    for hd in range(FX_HEADS):
        grp = hd // 2
        zq_h = zq[:, grp * LANES:(grp + 1) * LANES]
        zk_h = zk[:, grp * LANES:(grp + 1) * LANES]
        if hd % 2:
            zq_h = pltpu.roll(zq_h, LANES // 2, 1)
            zk_h = pltpu.roll(zk_h, LANES // 2, 1)
        hot = [(lane_row == base0 + 8 * j + hd) for j in range(3) for base0 in (_AUG0, _AUG0 + 24)]
        hot_k = (hot[0] | hot[2] | hot[4]).astype(F32)
        hot_q = (hot[1] | hot[3] | hot[5]).astype(F32)
        fq_ref[0, hd] = jnp.where(lane < _AUG0, zq_h, cq + hot_q).astype(BF16)
        fk_ref[0, hd] = jnp.where(lane < _AUG0, zk_h, ck + hot_k).astype(BF16)

    zqk = proj(_C_QK, _C_V)
    zbuf_ref[_TAIL:_TAIL + tm, :] = zqk
    conv = cb_ref[...] + zqk * cw_ref[CONV_W - 1:CONV_W, :]
    for j in range(CONV_W - 1):
        off = _TAIL - (CONV_W - 1) + j
        conv = conv + zbuf_ref[off:off + tm, :] * cw_ref[j:j + 1, :]
    zbuf_ref[0:_TAIL, :] = zbuf_ref[tm:tm + _TAIL, :]
    act = conv * jax.nn.sigmoid(conv)
    mq_ref[0] = act[:, 0:ML_W].astype(BF16)
    mk_ref[0] = (act[:, ML_W:2 * ML_W] * (ML_DH ** -0.5)).astype(BF16)

    zt = lax.dot_general(wt_ref[...], h, (((1,), (1,)), ((), ())), preferred_element_type=F32)
    fvt_ref[0, 0] = zt[0:FX_W, :].astype(BF16)
    gt = zt[_R_GT:_R_GT + _N_GT, :] + bcol_ref[...]
    glt = _log_sigmoid(gt)
    row = lax.broadcasted_iota(jnp.int32, (_N_GT, L), 0)
    triu = (lax.broadcasted_iota(jnp.int32, (L, L), 0)
            <= lax.broadcasted_iota(jnp.int32, (L, L), 1)).astype(BF16)
    for ci in range(tm // L):
        cols = slice(ci * L, (ci + 1) * L)
        brow = sum(jnp.dot(part[:, cols], triu, preferred_element_type=F32)
                   for part in _split3(glt))
        gatet_ref[0, :, cols] = LOG2E * jnp.where(row < ML_HEADS, gt[:, cols], brow)

    v_ref[0] = proj(_C_V, _C_O).astype(BF16)
    o_ref[0] = proj(_C_O, _C_FQ).astype(BF16)


def _proj(x, ln1, w_main, w_t, brow, bcol, conv_w, conv_b, *, tm, tkc, L):
    B, S, _ = x.shape
    ns = S // tm
    r = tkc // tm
    kern = functools.partial(_proj_kernel, tm=tm, L=L)
    row_spec = lambda n: pl.BlockSpec((1, tm, n), lambda b, s: (b, s, 0))
    out_shape = (
        jax.ShapeDtypeStruct((B, S, ML_W), BF16),
        jax.ShapeDtypeStruct((B, S, ML_W), BF16),
        jax.ShapeDtypeStruct((B, S, ML_W), BF16),
        jax.ShapeDtypeStruct((B, S, ML_W), BF16),
        jax.ShapeDtypeStruct((B, S, LANES), F32),
        jax.ShapeDtypeStruct((B, _N_GT, S), F32),
        jax.ShapeDtypeStruct((B, FX_HEADS, S, LANES), BF16),
        jax.ShapeDtypeStruct((B, FX_HEADS, S, LANES), BF16),
        jax.ShapeDtypeStruct((B, S // tkc, FX_W, tkc), BF16),
    )
    out_specs = (
        row_spec(ML_W), row_spec(ML_W), row_spec(ML_W), row_spec(ML_W), row_spec(LANES),
        pl.BlockSpec((1, _N_GT, tm), lambda b, s: (b, 0, s)),
        pl.BlockSpec((1, FX_HEADS, tm, LANES), lambda b, s: (b, 0, s, 0)),
        pl.BlockSpec((1, FX_HEADS, tm, LANES), lambda b, s: (b, 0, s, 0)),
        pl.BlockSpec((1, 1, FX_W, tm), lambda b, s: (b, s // r, 0, s % r)),
    )
    return pl.pallas_call(
        kern,
        grid=(B, ns),
        in_specs=[
            row_spec(D_MODEL),
            _const_spec((1, D_MODEL)),
            _const_spec(w_main.shape),
            _const_spec(w_t.shape),
            _const_spec((1, LANES)),
            _const_spec((_N_GT, 1)),
            _const_spec((CONV_W, 2 * ML_W)),
            _const_spec((1, 2 * ML_W)),
        ],
        out_specs=out_specs,
        out_shape=out_shape,
        scratch_shapes=[pltpu.VMEM((1, LANES), F32),
                        pltpu.VMEM((tm + _TAIL, 2 * ML_W), F32)],
        compiler_params=pltpu.CompilerParams(
            dimension_semantics=("arbitrary", "arbitrary"), vmem_limit_bytes=VMEM_LIMIT),
        name="proj",
    )(x, ln1, w_main, w_t, brow, bcol, conv_w, conv_b)


_ML_SL = [slice(hd * ML_DH, (hd + 1) * ML_DH) for hd in range(ML_HEADS)]
_V_INTER, _V_EM, _V_WA, _V_DECAY = 0, 1, 2, 3


def _mlstm_gates(gate, gatet, m_ref, et_ref, vec_ref, slot, *, L):
    src_i = lax.broadcasted_iota(jnp.int32, (L, L), 0)
    tgt_i = lax.broadcasted_iota(jnp.int32, (L, L), 1)
    causal_t = src_i <= tgt_i
    for hd in range(ML_HEADS):
        ig_row = gatet[hd:hd + 1, :]
        b_row = gatet[ML_HEADS + hd:ML_HEADS + hd + 1, :]
        r_col = gate[:, hd:hd + 1] - gate[:, ML_HEADS + hd:ML_HEADS + hd + 1]
        g = b_row[:, L - 1:L]
        m_prev = m_ref[hd:hd + 1, 0:1]
        dmat_t = jnp.where(causal_t, b_row + r_col, NEG_BIG)
        m_inter = b_row + m_prev
        m_t = jnp.maximum(m_inter, jnp.max(dmat_t, axis=0, keepdims=True))
        a_row = g - b_row + ig_row
        m_new = jnp.maximum(g + m_prev, jnp.max(a_row, axis=1, keepdims=True))
        m_ref[hd:hd + 1, :] = jnp.broadcast_to(m_new, (1, LANES))
        et_ref[slot, hd] = jnp.exp2(dmat_t - m_t)
        vec_ref[slot, hd, _V_INTER:_V_INTER + 1, :] = jnp.exp2(m_inter - m_t)
        vec_ref[slot, hd, _V_EM:_V_EM + 1, :] = jnp.exp2(-m_t)
        vec_ref[slot, hd, _V_WA:_V_WA + 1, :] = jnp.exp2(a_row - m_new)
        vec_ref[slot, hd, _V_DECAY:_V_DECAY + 1, :] = jnp.broadcast_to(
            jnp.exp2(g + m_prev - m_new), (1, L))


def _mlstm_mix(q_ref, k_ref, v_ref, et_ref, vec_ref, state_ref, hvt_ref, slot, *, L):
    one_rows = (lax.broadcasted_iota(jnp.int32, (ML_DH, L), 0) == 0).astype(BF16)
    heads = range(ML_HEADS)
    vec = [vec_ref[slot, hd] for hd in heads]
    q = [q_ref[0, :, _ML_SL[hd]] for hd in heads]
    k = [k_ref[0, :, _ML_SL[hd]] for hd in heads]
    vaug_t = [jnp.concatenate([v_ref[0, :, _ML_SL[hd]].T, one_rows], axis=0)
              for hd in heads]
    vaug_tw = [(vaug_t[hd].astype(F32) * vec[hd][_V_WA:_V_WA + 1, :]).astype(BF16) for hd in heads]
    s_t = [lax.dot_general(k[hd], q[hd], (((1,), (1,)), ((), ())),
                           preferred_element_type=F32) for hd in heads]
    qs_t = [lax.dot_general(state_ref[hd].astype(BF16), q[hd], (((1,), (1,)), ((), ())),
                            preferred_element_type=F32) for hd in heads]
    upd = [jnp.dot(vaug_tw[hd], k[hd], preferred_element_type=F32) for hd in heads]
    for hd in heads:
        state_ref[hd] = vec[hd][_V_DECAY:_V_DECAY + 1, 0:1] * state_ref[hd] + upd[hd]
    scores_t = [(s_t[hd] * et_ref[slot, hd]).astype(BF16) for hd in heads]
    nd_t = [jnp.dot(vaug_t[hd], scores_t[hd], preferred_element_type=F32)
            + vec[hd][_V_INTER:_V_INTER + 1, :] * qs_t[hd] for hd in heads]
    for hd in heads:
        den = nd_t[hd][ML_DH:ML_DH + 1, :]
        hvt_ref[slot, hd] = nd_t[hd][0:ML_DH, :] / jnp.maximum(
            jnp.abs(den), vec[hd][_V_EM:_V_EM + 1, :])


def _mlstm_out(hvt_ref, o_ref, nrm_ref, out_ref, slot):
    for hd in range(ML_HEADS):
        hv_t = hvt_ref[slot, hd]
        hn_t = hv_t * lax.rsqrt(jnp.mean(hv_t * hv_t, axis=0, keepdims=True) + EPS)
        og = jax.nn.sigmoid(o_ref[0, :, _ML_SL[hd]].astype(F32))
        out_ref[0, :, _ML_SL[hd]] = (hn_t.T * nrm_ref[:, _ML_SL[hd]] * og).astype(BF16)


def _mlstm_kernel(q_ref, k_ref, v_ref, o_ref, gate0_ref, gatet0_ref, gate_ref, gatet_ref, nrm_ref,
                  out_ref, state_ref, m_ref, e_ref, vec_ref, hv_ref, *, L):
    c = pl.program_id(1)

    @pl.when(c == 0)
    def _():
        state_ref[...] = jnp.zeros_like(state_ref)
        m_ref[...] = jnp.zeros_like(m_ref)
        hv_ref[...] = jnp.zeros_like(hv_ref)
        _mlstm_gates(gate0_ref[0], gatet0_ref[0], m_ref, e_ref, vec_ref, 0, L=L)

    slot = c % 2
    _mlstm_out(hv_ref, o_ref, nrm_ref, out_ref, 1 - slot)
    _mlstm_mix(q_ref, k_ref, v_ref, e_ref, vec_ref, state_ref, hv_ref, slot, L=L)
    _mlstm_gates(gate_ref[0], gatet_ref[0], m_ref, e_ref, vec_ref, 1 - slot, L=L)


def _mlstm(q, k, v, o, gate, gatet, nrm, *, L):
    B, S, _ = q.shape
    nc = S // L
    kern = functools.partial(_mlstm_kernel, L=L)
    cur = lambda b, c: (b, jnp.minimum(c, nc - 1), 0)
    prev = lambda b, c: (b, jnp.maximum(c - 1, 0), 0)
    row_spec = lambda n, im: pl.BlockSpec((1, L, n), im)
    return pl.pallas_call(
        kern,
        grid=(B, nc + 1),
        in_specs=[
            row_spec(ML_W, cur), row_spec(ML_W, cur), row_spec(ML_W, cur), row_spec(ML_W, prev),
            row_spec(LANES, lambda b, c: (b, 0, 0)),
            pl.BlockSpec((1, _N_GT, L), lambda b, c: (b, 0, 0)),
            row_spec(LANES, lambda b, c: (b, jnp.minimum(c + 1, nc - 1), 0)),
            pl.BlockSpec((1, _N_GT, L), lambda b, c: (b, 0, jnp.minimum(c + 1, nc - 1))),
            _const_spec((1, ML_W)),
        ],
        out_specs=row_spec(ML_W, prev),
        out_shape=jax.ShapeDtypeStruct((B, S, ML_W), BF16),
        scratch_shapes=[
            pltpu.VMEM((ML_HEADS, 2 * ML_DH, ML_DH), F32),
            pltpu.VMEM((8, LANES), F32),
            pltpu.VMEM((2, ML_HEADS, L, L), F32),
            pltpu.VMEM((2, ML_HEADS, 4, L), F32),
            pltpu.VMEM((2, ML_HEADS, ML_DH, L), F32),
        ],
        compiler_params=pltpu.CompilerParams(
            dimension_semantics=("arbitrary", "arbitrary"), vmem_limit_bytes=VMEM_LIMIT),
        name="mlstm",
    )(q, k, v, o, gate, gatet, gate, gatet, nrm)


_HP = 4
_VROWS = FX_DH + 16
_LAG = 3 * _HP


def _fox_kernel(q_ref, k_ref, vt_ref, o_ref, m_ref, acc_ref, p_ref, *, tq, tkc):
    qi = pl.program_id(2)
    nsub = tkc // tq
    n_full = qi // nsub
    rem = qi % nsub
    key_i = lax.broadcasted_iota(jnp.int32, (tq, tq), 0)
    qry_i = lax.broadcasted_iota(jnp.int32, (tq, tq), 1)

    m_ref[...] = jnp.full_like(m_ref, NEG_BIG)
    acc_ref[...] = jnp.zeros_like(acc_ref)
    ones_rows = (lax.broadcasted_iota(jnp.int32, (_VROWS - FX_DH, tq), 0) == 0).astype(BF16)

    def chunk(j, n_plain, diagonal):
        state = [(m_ref[hd], acc_ref[hd]) for hd in range(_HP)]
        tiles = [(i, hd) for i in range(n_plain + int(diagonal)) for hd in range(_HP)]
        scores = {}

        def qk(i, hd):
            st = lax.dot_general(k_ref[0, hd, j, i * tq:(i + 1) * tq, :], q_ref[0, hd],
                                 (((1,), (1,)), ((), ())),
                                 preferred_element_type=F32)
            if diagonal and i == n_plain:
                st = jnp.where(qry_i >= key_i, st, NEG_BIG)
            scores[i, hd] = st

        alphas = {}

        def softmax(t, i, hd):
            m, acc = state[hd]
            st = scores.pop((i, hd))
            m_new = jnp.maximum(m, jnp.max(st, axis=0, keepdims=True))
            alphas[i, hd] = jnp.exp2(m - m_new)
            p_ref[t] = jnp.exp2(st - m_new).astype(BF16)
            state[hd] = (m_new, acc)

        def pv(t, i, hd):
            m, acc = state[hd]
            vt = jnp.concatenate(
                [vt_ref[0, j, hd * FX_DH:(hd + 1) * FX_DH, i * tq:(i + 1) * tq], ones_rows],
                axis=0)
            acc = alphas.pop((i, hd)) * acc + jnp.dot(vt, p_ref[t],
                                                      preferred_element_type=F32)
            state[hd] = (m, acc)

        for t in range(len(tiles) + _LAG):
            if t < len(tiles):
                qk(*tiles[t])
            if _HP <= t < len(tiles) + _HP:
                softmax(t - _HP, *tiles[t - _HP])
            if t >= _LAG:
                pv(t - _LAG, *tiles[t - _LAG])
        for hd in range(_HP):
            m_ref[hd], acc_ref[hd] = state[hd]

    def body(j, _):
        chunk(j, nsub, False)
        return 0

    lax.fori_loop(0, n_full, body, 0)
    for r in range(nsub):
        pl.when(rem == r)(functools.partial(chunk, n_full, r, True))
    out_t = jnp.concatenate(
        [acc_ref[hd, 0:FX_DH, :] / acc_ref[hd, FX_DH:FX_DH + 1, :] for hd in range(_HP)],
        axis=0)
    o_ref[0] = out_t.T.astype(BF16)


def _fox(fq, fk, fvt, *, tq, tkc):
    B, H, S, _ = fq.shape
    n = S // tkc
    fk5 = fk.reshape(B, H, n, tkc, LANES)
    kern = functools.partial(_fox_kernel, tq=tq, tkc=tkc)
    return pl.pallas_call(
        kern,
        grid=(B, H // _HP, S // tq),
        in_specs=[
            pl.BlockSpec((1, _HP, tq, LANES), lambda b, h, q: (b, h, q, 0)),
            pl.BlockSpec((1, _HP, n, tkc, LANES), lambda b, h, q: (b, h, 0, 0, 0)),
            pl.BlockSpec((1, n, _HP * FX_DH, tkc), lambda b, h, q: (b, 0, h, 0)),
        ],
        out_specs=pl.BlockSpec((1, tq, _HP * FX_DH), lambda b, h, q: (b, q, h)),
        out_shape=jax.ShapeDtypeStruct((B, S, FX_W), BF16),
        scratch_shapes=[
            pltpu.VMEM((_HP, 1, tq), F32),
            pltpu.VMEM((_HP, _VROWS, tq), F32),
            pltpu.VMEM((tkc // tq * _HP, tq, tq), BF16),
        ],
        compiler_params=pltpu.CompilerParams(
            dimension_semantics=("arbitrary", "arbitrary", "arbitrary"),
            vmem_limit_bytes=VMEM_LIMIT),
        name="fox",
    )(fq, fk5, fvt)


def _memkv_kernel(mem_ref, g_ref, w_ref, k_ref, v_ref):
    mn = _rms(mem_ref[0], g_ref[...]).astype(BF16)
    k_ref[0] = jnp.dot(mn, w_ref[:, 0:D_MODEL], preferred_element_type=F32).astype(BF16)
    v_ref[0] = jnp.dot(mn, w_ref[:, D_MODEL:2 * D_MODEL], preferred_element_type=F32).astype(BF16)


def _memkv(mem, ln_mem, w_xkv):
    B, M, _ = mem.shape
    spec = pl.BlockSpec((1, M, D_MODEL), lambda b: (b, 0, 0))
    return pl.pallas_call(
        _memkv_kernel,
        grid=(B,),
        in_specs=[spec, _const_spec((1, D_MODEL)), _const_spec(w_xkv.shape)],
        out_specs=(spec, spec),
        out_shape=(jax.ShapeDtypeStruct((B, M, D_MODEL), BF16),) * 2,
        compiler_params=pltpu.CompilerParams(
            dimension_semantics=("arbitrary",), vmem_limit_bytes=VMEM_LIMIT),
        name="memkv",
    )(mem, ln_mem, w_xkv)


_ROW_GROUPS = 2


def _xattn_kernel(x_ref, ml_ref, fx_ref, wout_ref, lnx_ref, wq_ref, km_ref, vm_ref, wo_ref,
                  out_ref):
    tm = x_ref.shape[1]
    groups = [slice(r, r + tm // _ROW_GROUPS) for r in range(0, tm, tm // _ROW_GROUPS)]
    sls = [slice(hd * X_DH, (hd + 1) * X_DH) for hd in range(X_HEADS)]
    km = km_ref[0]
    vm = vm_ref[0]
    x1 = [x_ref[0, g, :]
          + jnp.dot(ml_ref[0, g, :], wout_ref[0:ML_W, :], preferred_element_type=F32)
          + jnp.dot(fx_ref[0, g, :], wout_ref[ML_W:ML_W + FX_W, :], preferred_element_type=F32)
          for g in groups]
    q = [(jnp.dot(_rms(x, lnx_ref[...]).astype(BF16), wq_ref[...], preferred_element_type=F32)
          * (X_DH ** -0.5)).astype(BF16) for x in x1]
    s = [[lax.dot_general(qg[:, sl], km[:, sl], (((1,), (1,)), ((), ())),
                          preferred_element_type=F32) for sl in sls] for qg in q]
    o = []
    for sg in s:
        heads = []
        for hd, sl in enumerate(sls):
            p = jnp.exp(sg[hd] - jnp.max(sg[hd], axis=-1, keepdims=True))
            l = jnp.sum(p, axis=-1, keepdims=True)
            oh = jnp.dot(p.astype(BF16), vm[:, sl], preferred_element_type=F32) / l
            heads.append(oh.astype(BF16))
        o.append(jnp.concatenate(heads, axis=1))
    for g, xg, og in zip(groups, x1, o):
        out_ref[0, g, :] = xg + jnp.dot(og, wo_ref[...], preferred_element_type=F32)


def _xattn(x, ml, fx, w_out, ln_x, w_xq, kmem, vmem, w_xo, *, tm):
    B, S, _ = x.shape
    M = kmem.shape[1]
    row_spec = lambda n: pl.BlockSpec((1, tm, n), lambda b, s: (b, s, 0))
    mem_spec = pl.BlockSpec((1, M, D_MODEL), lambda b, s: (b, 0, 0))
    return pl.pallas_call(
        _xattn_kernel,
        grid=(B, S // tm),
        in_specs=[
            row_spec(D_MODEL), row_spec(ML_W), row_spec(FX_W),
            _const_spec(w_out.shape), _const_spec((1, D_MODEL)), _const_spec(w_xq.shape),
            mem_spec, mem_spec, _const_spec(w_xo.shape),
        ],
        out_specs=row_spec(D_MODEL),
        out_shape=jax.ShapeDtypeStruct((B, S, D_MODEL), F32),
        compiler_params=pltpu.CompilerParams(
            dimension_semantics=("arbitrary", "arbitrary"), vmem_limit_bytes=VMEM_LIMIT),
        name="xattn",
    )(x, ml, fx, w_out, ln_x, w_xq, kmem, vmem, w_xo)


_FF_CHUNK = 1024


def _mlp_kernel(x_ref, ln2_ref, w1_ref, w2_ref, lnf_ref, out_ref):
    x = x_ref[0]
    h = _rms(x, ln2_ref[...]).astype(BF16)
    acc = x
    for c in range(D_FF // _FF_CHUNK):
        sl = slice(c * _FF_CHUNK, (c + 1) * _FF_CHUNK)
        u = jnp.maximum(jnp.dot(h, w1_ref[:, sl], preferred_element_type=F32), 0.0)
        acc = acc + jnp.dot((u * u).astype(BF16), w2_ref[sl, :], preferred_element_type=F32)
    out_ref[0] = _rms(acc, lnf_ref[...])


def _mlp(x, ln2, w1, w2, ln_f, *, tm):
    B, S, _ = x.shape
    row_spec = pl.BlockSpec((1, tm, D_MODEL), lambda b, s: (b, s, 0))
    return pl.pallas_call(
        _mlp_kernel,
        grid=(B, S // tm),
        in_specs=[row_spec, _const_spec((1, D_MODEL)), _const_spec(w1.shape),
                  _const_spec(w2.shape), _const_spec((1, D_MODEL))],
        out_specs=row_spec,
        out_shape=jax.ShapeDtypeStruct((B, S, D_MODEL), F32),
        compiler_params=pltpu.CompilerParams(
            dimension_semantics=("arbitrary", "arbitrary"), vmem_limit_bytes=VMEM_LIMIT),
        name="mlp",
    )(x, ln2, w1, w2, ln_f)


def _layer(x, mem, ln1, w_in, conv_w, conv_b, b_i, b_f, ml_norm, fx_b_f, w_out,
           ln_x, ln_mem, w_xq, w_xkv, w_xo, ln2, w_ff1, w_ff2, ln_f_or_none):
    B, S, _ = x.shape
    t = _tiles(S)
    o_qk, o_v, o_o = 0, 2 * ML_W, 3 * ML_W
    o_i = 4 * ML_W
    o_f = o_i + ML_HEADS
    o_fq = o_f + ML_HEADS
    o_fk, o_fv = o_fq + FX_W, o_fq + 2 * FX_W
    o_ff = o_fq + 3 * FX_W
    w_gate = jnp.concatenate(
        [w_in[:, o_i:o_fq], w_in[:, o_ff:o_ff + FX_HEADS]], axis=1)
    w_main = jnp.concatenate(
        [w_in[:, o_qk:o_i], w_in[:, o_fq:o_fv],
         jnp.pad(w_gate, ((0, 0), (0, LANES - _N_GT)))], axis=1).astype(BF16)
    w_t = jnp.concatenate([w_in[:, o_fv:o_ff], w_gate], axis=1).T.astype(BF16)
    bias = jnp.concatenate([b_i, b_f, fx_b_f]).astype(F32)
    brow = jnp.pad(bias, (0, LANES - _N_GT)).reshape(1, LANES)
    bcol = bias.reshape(_N_GT, 1)

    mq, mk, v, o, gate, gatet, fq, fk, fvt = _proj(
        x, ln1.reshape(1, -1), w_main, w_t, brow, bcol, conv_w, conv_b.reshape(1, -1),
        tm=t["tm"], tkc=t["tkc"], L=t["chunk"])
    ml = _mlstm(mq, mk, v, o, gate, gatet, ml_norm.reshape(1, -1), L=t["chunk"])
    fx = _fox(fq, fk, fvt, tq=t["tq"], tkc=t["tkc"])
    kmem, vmem = _memkv(mem, ln_mem.reshape(1, -1), w_xkv.astype(BF16))
    x2 = _xattn(x, ml, fx, w_out.astype(BF16), ln_x.reshape(1, -1), w_xq.astype(BF16),
                kmem, vmem, w_xo.astype(BF16), tm=t["tail"])
    return _mlp(x2, ln2.reshape(1, -1), w_ff1.astype(BF16), w_ff2.astype(BF16),
                ln_f_or_none.reshape(1, -1), tm=t["tail"])


def kernel(x, mem, ln1, w_in, ml_conv_w, ml_conv_b, ml_b_i, ml_b_f, ml_norm, fx_b_f, w_out,
           ln_x, ln_mem, w_xq, w_xkv, w_xo, ln2, w_ff1, w_ff2, ln_f):
    depth = w_in.shape[0]
    assert depth == 1, "single-layer problem: the final norm is fused into the layer's MLP kernel"
    return _layer(x, mem, ln1[0], w_in[0], ml_conv_w[0], ml_conv_b[0], ml_b_i[0], ml_b_f[0],
                  ml_norm[0], fx_b_f[0], w_out[0], ln_x[0], ln_mem[0], w_xq[0], w_xkv[0],
                  w_xo[0], ln2[0], w_ff1[0], w_ff2[0], ln_f)
```

```python
import functools

import jax
import jax.numpy as jnp
from jax import lax
from jax.experimental import pallas as pl
from jax.experimental.pallas import tpu as pltpu

F32 = jnp.float32
BF16 = jnp.bfloat16

D_MODEL = 1024
ML_HEADS = 4
ML_DH = 128
ML_W = ML_HEADS * ML_DH
CONV_W = 4
FX_HEADS = 8
FX_DH = 64
FX_W = FX_HEADS * FX_DH
X_HEADS = 4
X_DH = D_MODEL // X_HEADS
D_FF = 4 * D_MODEL
EPS = 1e-6

LANES = 128
NEG_BIG = -1e30
LOG2E = 1.4426950408889634
_TAIL = 8
VMEM_LIMIT = 56 * 1024 * 1024

_C_QK = 0
_C_V = _C_QK + 2 * ML_W
_C_O = _C_V + ML_W
_C_FQ = _C_O + ML_W
_C_FK = _C_FQ + FX_W
_C_G = _C_FK + FX_W
_C_END = _C_G + LANES
_R_GT = FX_W
_N_GT = 16

_AUG0 = FX_DH


def _tiles(S):
    tm = min(512, S)
    return dict(
        tm=tm,
        chunk=min(256, S),
        ml_rows=4,
        tq=min(256, S),
        qb=min(8, S // min(256, S)),
        tkc=max(tm, min(2048, S)),
        tail=min(1024, S),
    )


def _rms(x, g):
    return x * lax.rsqrt(jnp.mean(x * x, axis=-1, keepdims=True) + EPS) * g


def _split3(x):
    hi = x.astype(BF16)
    r = x - hi.astype(F32)
    mid = r.astype(BF16)
    lo = (r - mid.astype(F32)).astype(BF16)
    return hi, mid, lo


def _log_sigmoid(x):
    return jnp.minimum(x, 0.0) - jnp.log1p(jnp.exp(-jnp.abs(x)))


def _const_spec(shape):
    nd = len(shape)
    return pl.BlockSpec(shape, lambda *_: (0,) * nd, pipeline_mode=pl.Buffered(1))


def _proj_kernel(x_ref, g_ref, w_ref, wt_ref, brow_ref, bcol_ref, cw_ref, cb_ref,
                 mq_ref, mk_ref, v_ref, o_ref, gate_ref, gatet_ref, fq_ref, fk_ref, fvt_ref,
                 carry_ref, zbuf_ref, *, tm, L):
    @pl.when(pl.program_id(1) == 0)
    def _():
        carry_ref[...] = jnp.zeros_like(carry_ref)
        zbuf_ref[0:_TAIL, :] = jnp.zeros((_TAIL, 2 * ML_W), F32)

    x = x_ref[0]
    h = _rms(x, g_ref[...]).astype(BF16)

    def proj(c0, c1):
        return jnp.dot(h, w_ref[:, c0:c1], preferred_element_type=F32)

    lane = lax.broadcasted_iota(jnp.int32, (tm, LANES), 1)

    def in_lanes(lo, n):
        return (lane >= lo) & (lane < lo + n)

    gpre = proj(_C_G, _C_END) + brow_ref[...]
    zq = proj(_C_FQ, _C_FK) * (FX_DH ** -0.5 * LOG2E)
    zk = proj(_C_FK, _C_G)
    gls = _log_sigmoid(gpre)
    tril = (lax.broadcasted_iota(jnp.int32, (L, L), 1)
            <= lax.broadcasted_iota(jnp.int32, (L, L), 0)).astype(BF16)
    is_ig = lax.broadcasted_iota(jnp.int32, (L, LANES), 1) < ML_HEADS
    parts = _split3(gls)
    carry = carry_ref[...]
    csum_chunks = []
    for ci in range(tm // L):
        rows = slice(ci * L, (ci + 1) * L)
        local = sum(jnp.dot(tril, part[rows], preferred_element_type=F32) for part in parts)
        gate_ref[0, rows, :] = LOG2E * jnp.where(is_ig, gpre[rows], local)
        csum_chunks.append(carry + local)
        carry = carry + local[L - 1:L, :]
    carry_ref[...] = carry
    csum = jnp.concatenate(csum_chunks, axis=0)

    c_hi, c_mid, c_lo = (p.astype(F32) for p in _split3(csum * LOG2E))
    src = 2 * ML_HEADS
    cq = jnp.where(in_lanes(_AUG0, 8), pltpu.roll(c_hi, _AUG0 - src, 1),
         jnp.where(in_lanes(_AUG0 + 8, 8), pltpu.roll(c_mid, _AUG0 + 8 - src, 1),
         jnp.where(in_lanes(_AUG0 + 16, 8), pltpu.roll(c_lo, _AUG0 + 16 - src, 1), 0.0)))
    ck = -pltpu.roll(cq, 24, 1)
    lane_row = lax.broadcasted_iota(jnp.int32, (1, LANES), 1)
    for hd in range(FX_HEADS):
        grp = hd // 2
        zq_h = zq[:, grp * LANES:(grp + 1) * LANES]
        zk_h = zk[:, grp * LANES:(grp + 1) * LANES]
        if hd % 2:
            zq_h = pltpu.roll(zq_h, LANES // 2, 1)
            zk_h = pltpu.roll(zk_h, LANES // 2, 1)
        hot = [(lane_row == base0 + 8 * j + hd) for j in range(3) for base0 in (_AUG0, _AUG0 + 24)]
        hot_k = (hot[0] | hot[2] | hot[4]).astype(F32)
        hot_q = (hot[1] | hot[3] | hot[5]).astype(F32)
        fq_ref[0, hd] = jnp.where(lane < _AUG0, zq_h, cq + hot_q).astype(BF16)
        fk_ref[0, hd] = jnp.where(lane < _AUG0, zk_h, ck + hot_k).astype(BF16)

    zqk = proj(_C_QK, _C_V)
    zbuf_ref[_TAIL:_TAIL + tm, :] = zqk
    conv = cb_ref[...] + zqk * cw_ref[CONV_W - 1:CONV_W, :]
    for j in range(CONV_W - 1):
        off = _TAIL - (CONV_W - 1) + j
        conv = conv + zbuf_ref[off:off + tm, :] * cw_ref[j:j + 1, :]
    zbuf_ref[0:_TAIL, :] = zbuf_ref[tm:tm + _TAIL, :]
    act = conv * jax.nn.sigmoid(conv)
    mq_ref[0] = act[:, 0:ML_W].astype(BF16)
    mk_ref[0] = (act[:, ML_W:2 * ML_W] * (ML_DH ** -0.5)).astype(BF16)

    zt = lax.dot_general(wt_ref[...], h, (((1,), (1,)), ((), ())), preferred_element_type=F32)
    fvt_ref[0, 0] = zt[0:FX_W, :].astype(BF16)
    gt = zt[_R_GT:_R_GT + _N_GT, :] + bcol_ref[...]
    glt = _log_sigmoid(gt)
    row = lax.broadcasted_iota(jnp.int32, (_N_GT, L), 0)
    triu = (lax.broadcasted_iota(jnp.int32, (L, L), 0)
            <= lax.broadcasted_iota(jnp.int32, (L, L), 1)).astype(BF16)
    for ci in range(tm // L):
        cols = slice(ci * L, (ci + 1) * L)
        brow = sum(jnp.dot(part[:, cols], triu, preferred_element_type=F32)
                   for part in _split3(glt))
        gatet_ref[0, :, cols] = LOG2E * jnp.where(row < ML_HEADS, gt[:, cols], brow)

    v_ref[0] = proj(_C_V, _C_O).astype(BF16)
    o_ref[0] = proj(_C_O, _C_FQ).astype(BF16)


def _proj(x, ln1, w_main, w_t, brow, bcol, conv_w, conv_b, *, tm, tkc, L):
    B, S, _ = x.shape
    ns = S // tm
    r = tkc // tm
    kern = functools.partial(_proj_kernel, tm=tm, L=L)
    row_spec = lambda n: pl.BlockSpec((1, tm, n), lambda b, s: (b, s, 0))
    out_shape = (
        jax.ShapeDtypeStruct((B, S, ML_W), BF16),
        jax.ShapeDtypeStruct((B, S, ML_W), BF16),
        jax.ShapeDtypeStruct((B, S, ML_W), BF16),
        jax.ShapeDtypeStruct((B, S, ML_W), BF16),
        jax.ShapeDtypeStruct((B, S, LANES), F32),
        jax.ShapeDtypeStruct((B, _N_GT, S), F32),
        jax.ShapeDtypeStruct((B, FX_HEADS, S, LANES), BF16),
        jax.ShapeDtypeStruct((B, FX_HEADS, S, LANES), BF16),
        jax.ShapeDtypeStruct((B, S // tkc, FX_W, tkc), BF16),
    )
    out_specs = (
        row_spec(ML_W), row_spec(ML_W), row_spec(ML_W), row_spec(ML_W), row_spec(LANES),
        pl.BlockSpec((1, _N_GT, tm), lambda b, s: (b, 0, s)),
        pl.BlockSpec((1, FX_HEADS, tm, LANES), lambda b, s: (b, 0, s, 0)),
        pl.BlockSpec((1, FX_HEADS, tm, LANES), lambda b, s: (b, 0, s, 0)),
        pl.BlockSpec((1, 1, FX_W, tm), lambda b, s: (b, s // r, 0, s % r)),
    )
    return pl.pallas_call(
        kern,
        grid=(B, ns),
        in_specs=[
            row_spec(D_MODEL),
            _const_spec((1, D_MODEL)),
            _const_spec(w_main.shape),
            _const_spec(w_t.shape),
            _const_spec((1, LANES)),
            _const_spec((_N_GT, 1)),
            _const_spec((CONV_W, 2 * ML_W)),
            _const_spec((1, 2 * ML_W)),
        ],
        out_specs=out_specs,
        out_shape=out_shape,
        scratch_shapes=[pltpu.VMEM((1, LANES), F32),
                        pltpu.VMEM((tm + _TAIL, 2 * ML_W), F32)],
        compiler_params=pltpu.CompilerParams(
            dimension_semantics=("arbitrary", "arbitrary"), vmem_limit_bytes=VMEM_LIMIT),
        name="proj",
    )(x, ln1, w_main, w_t, brow, bcol, conv_w, conv_b)


_ML_SL = [slice(hd * ML_DH, (hd + 1) * ML_DH) for hd in range(ML_HEADS)]
_V_INTER, _V_EM, _V_WA, _V_DECAY = 0, 1, 2, 3


def _mlstm_gates(gate_ref, gatet_ref, m_ref, et_ref, vec_ref, slot, *, L, NB):
    src_i = lax.broadcasted_iota(jnp.int32, (L, L), 0)
    tgt_i = lax.broadcasted_iota(jnp.int32, (L, L), 1)
    causal_t = src_i <= tgt_i
    gates = [(gate_ref[nb], gatet_ref[nb]) for nb in range(NB)]
    for ch in range(NB * ML_HEADS):
        nb, hd = divmod(ch, ML_HEADS)
        gate, gatet = gates[nb]
        ig_row = gatet[hd:hd + 1, :]
        b_row = gatet[ML_HEADS + hd:ML_HEADS + hd + 1, :]
        r_col = gate[:, hd:hd + 1] - gate[:, ML_HEADS + hd:ML_HEADS + hd + 1]
        g = b_row[:, L - 1:L]
        m_prev = m_ref[ch:ch + 1, 0:1]
        dmat_t = jnp.where(causal_t, b_row + r_col, NEG_BIG)
        m_inter = b_row + m_prev
        m_t = jnp.maximum(m_inter, jnp.max(dmat_t, axis=0, keepdims=True))
        a_row = g - b_row + ig_row
        m_new = jnp.maximum(g + m_prev, jnp.max(a_row, axis=1, keepdims=True))
        m_ref[ch:ch + 1, :] = jnp.broadcast_to(m_new, (1, LANES))
        et_ref[slot, ch] = jnp.exp2(dmat_t - m_t)
        vec_ref[slot, ch, _V_INTER:_V_INTER + 1, :] = jnp.exp2(m_inter - m_t)
        vec_ref[slot, ch, _V_EM:_V_EM + 1, :] = jnp.exp2(-m_t)
        vec_ref[slot, ch, _V_WA:_V_WA + 1, :] = jnp.exp2(a_row - m_new)
        vec_ref[slot, ch, _V_DECAY:_V_DECAY + 1, :] = jnp.broadcast_to(
            jnp.exp2(g + m_prev - m_new), (1, L))


def _mlstm_mix(q_ref, k_ref, v_ref, et_ref, vec_ref, state_ref, hvt_ref, slot, *, L, NB):
    one_rows = (lax.broadcasted_iota(jnp.int32, (ML_DH, L), 0) == 0).astype(BF16)
    heads = range(NB * ML_HEADS)
    rows = lambda ref, ch: ref[ch // ML_HEADS, :, _ML_SL[ch % ML_HEADS]]
    vec = [vec_ref[slot, hd] for hd in heads]
    q = [rows(q_ref, hd) for hd in heads]
    k = [rows(k_ref, hd) for hd in heads]
    vaug_t = [jnp.concatenate([rows(v_ref, hd).T, one_rows], axis=0)
              for hd in heads]
    vaug_tw = [(vaug_t[hd].astype(F32) * vec[hd][_V_WA:_V_WA + 1, :]).astype(BF16) for hd in heads]
    s_t = [lax.dot_general(k[hd], q[hd], (((1,), (1,)), ((), ())),
                           preferred_element_type=F32) for hd in heads]
    qs_t = [lax.dot_general(state_ref[hd].astype(BF16), q[hd], (((1,), (1,)), ((), ())),
                            preferred_element_type=F32) for hd in heads]
    upd = [jnp.dot(vaug_tw[hd], k[hd], preferred_element_type=F32) for hd in heads]
    for hd in heads:
        state_ref[hd] = vec[hd][_V_DECAY:_V_DECAY + 1, 0:1] * state_ref[hd] + upd[hd]
    scores_t = [(s_t[hd] * et_ref[slot, hd]).astype(BF16) for hd in heads]
    nd_t = [jnp.dot(vaug_t[hd], scores_t[hd], preferred_element_type=F32)
            + vec[hd][_V_INTER:_V_INTER + 1, :] * qs_t[hd] for hd in heads]
    for hd in heads:
        den = nd_t[hd][ML_DH:ML_DH + 1, :]
        hvt_ref[slot, hd] = nd_t[hd][0:ML_DH, :] / jnp.maximum(
            jnp.abs(den), vec[hd][_V_EM:_V_EM + 1, :])


def _mlstm_out(hvt_ref, o_ref, nrm_ref, out_ref, slot, *, NB):
    for ch in range(NB * ML_HEADS):
        nb, hd = divmod(ch, ML_HEADS)
        hv_t = hvt_ref[slot, ch]
        hn_t = hv_t * lax.rsqrt(jnp.mean(hv_t * hv_t, axis=0, keepdims=True) + EPS)
        og = jax.nn.sigmoid(o_ref[nb, :, _ML_SL[hd]].astype(F32))
        out_ref[nb, :, _ML_SL[hd]] = (hn_t.T * nrm_ref[:, _ML_SL[hd]] * og).astype(BF16)


def _mlstm_kernel(q_ref, k_ref, v_ref, o_ref, gate0_ref, gatet0_ref, gate_ref, gatet_ref, nrm_ref,
                  out_ref, state_ref, m_ref, e_ref, vec_ref, hv_ref, *, L, NB):
    c = pl.program_id(1)

    @pl.when(c == 0)
    def _():
        state_ref[...] = jnp.zeros_like(state_ref)
        m_ref[...] = jnp.zeros_like(m_ref)
        hv_ref[...] = jnp.zeros_like(hv_ref)
        _mlstm_gates(gate0_ref, gatet0_ref, m_ref, e_ref, vec_ref, 0, L=L, NB=NB)

    slot = c % 2
    _mlstm_out(hv_ref, o_ref, nrm_ref, out_ref, 1 - slot, NB=NB)
    _mlstm_mix(q_ref, k_ref, v_ref, e_ref, vec_ref, state_ref, hv_ref, slot, L=L, NB=NB)
    _mlstm_gates(gate_ref, gatet_ref, m_ref, e_ref, vec_ref, 1 - slot, L=L, NB=NB)


def _mlstm(q, k, v, o, gate, gatet, nrm, *, L, NB):
    B, S, _ = q.shape
    nc = S // L
    nch = NB * ML_HEADS
    kern = functools.partial(_mlstm_kernel, L=L, NB=NB)
    cur = lambda b, c: (b, jnp.minimum(c, nc - 1), 0)
    prev = lambda b, c: (b, jnp.maximum(c - 1, 0), 0)
    row_spec = lambda n, im: pl.BlockSpec((NB, L, n), im)
    return pl.pallas_call(
        kern,
        grid=(B // NB, nc + 1),
        in_specs=[
            row_spec(ML_W, cur), row_spec(ML_W, cur), row_spec(ML_W, cur), row_spec(ML_W, prev),
            row_spec(LANES, lambda b, c: (b, 0, 0)),
            pl.BlockSpec((NB, _N_GT, L), lambda b, c: (b, 0, 0)),
            row_spec(LANES, lambda b, c: (b, jnp.minimum(c + 1, nc - 1), 0)),
            pl.BlockSpec((NB, _N_GT, L), lambda b, c: (b, 0, jnp.minimum(c + 1, nc - 1))),
            _const_spec((1, ML_W)),
        ],
        out_specs=row_spec(ML_W, prev),
        out_shape=jax.ShapeDtypeStruct((B, S, ML_W), BF16),
        scratch_shapes=[
            pltpu.VMEM((nch, 2 * ML_DH, ML_DH), F32),
            pltpu.VMEM((-(-nch // 8) * 8, LANES), F32),
            pltpu.VMEM((2, nch, L, L), F32),
            pltpu.VMEM((2, nch, 4, L), F32),
            pltpu.VMEM((2, nch, ML_DH, L), F32),
        ],
        compiler_params=pltpu.CompilerParams(
            dimension_semantics=("arbitrary", "arbitrary"), vmem_limit_bytes=VMEM_LIMIT),
        name="mlstm",
    )(q, k, v, o, gate, gatet, gate, gatet, nrm)


_HP = 4
_VROWS = FX_DH + 16
_LAG = 3 * _HP


def _fox_kernel(q_ref, k_ref, vt_ref, o_ref, m_ref, acc_ref, p_ref, *, tq, tkc, qb):
    nsub = tkc // tq
    key_i = lax.broadcasted_iota(jnp.int32, (tq, tq), 0)
    qry_i = lax.broadcasted_iota(jnp.int32, (tq, tq), 1)
    ones_rows = (lax.broadcasted_iota(jnp.int32, (_VROWS - FX_DH, tq), 0) == 0).astype(BF16)

    def query_block(qq, _):
        qi = pl.program_id(2) * qb + qq
        q_rows = pl.ds(pl.multiple_of(qq * tq, tq), tq)
        n_full = qi // nsub
        rem = qi % nsub

        m_ref[...] = jnp.full_like(m_ref, NEG_BIG)
        acc_ref[...] = jnp.zeros_like(acc_ref)

        def chunk(j, n_plain, diagonal):
            state = [(m_ref[hd], acc_ref[hd]) for hd in range(_HP)]
            tiles = [(i, hd) for i in range(n_plain + int(diagonal)) for hd in range(_HP)]
            scores = {}

            def qk(i, hd):
                st = lax.dot_general(k_ref[0, hd, j, i * tq:(i + 1) * tq, :],
                                     q_ref[0, hd, q_rows, :], (((1,), (1,)), ((), ())),
                                     preferred_element_type=F32)
                if diagonal and i == n_plain:
                    st = jnp.where(qry_i >= key_i, st, NEG_BIG)
                scores[i, hd] = st

            alphas = {}

            def softmax(t, i, hd):
                m, acc = state[hd]
                st = scores.pop((i, hd))
                m_new = jnp.maximum(m, jnp.max(st, axis=0, keepdims=True))
                alphas[i, hd] = jnp.exp2(m - m_new)
                p_ref[t] = jnp.exp2(st - m_new).astype(BF16)
                state[hd] = (m_new, acc)

            def pv(t, i, hd):
                m, acc = state[hd]
                vt = jnp.concatenate(
                    [vt_ref[0, j, hd * FX_DH:(hd + 1) * FX_DH, i * tq:(i + 1) * tq], ones_rows],
                    axis=0)
                acc = alphas.pop((i, hd)) * acc + jnp.dot(
                    vt, p_ref[t], preferred_element_type=F32)
                state[hd] = (m, acc)

            for t in range(len(tiles) + _LAG):
                if t < len(tiles):
                    qk(*tiles[t])
                if _HP <= t < len(tiles) + _HP:
                    softmax(t - _HP, *tiles[t - _HP])
                if t >= _LAG:
                    pv(t - _LAG, *tiles[t - _LAG])
            for hd in range(_HP):
                m_ref[hd], acc_ref[hd] = state[hd]

        def body(j, _):
            chunk(j, nsub, False)
            return 0

        lax.fori_loop(0, n_full, body, 0)
        for r in range(nsub):
            pl.when(rem == r)(functools.partial(chunk, n_full, r, True))
        out_t = jnp.concatenate(
            [acc_ref[hd, 0:FX_DH, :] / acc_ref[hd, FX_DH:FX_DH + 1, :] for hd in range(_HP)],
            axis=0)
        o_ref[0, q_rows, :] = out_t.T.astype(BF16)
        return 0

    lax.fori_loop(0, qb, query_block, 0)


def _fox(fq, fk, fvt, *, tq, tkc, qb):
    B, H, S, _ = fq.shape
    n = S // tkc
    fk5 = fk.reshape(B, H, n, tkc, LANES)
    kern = functools.partial(_fox_kernel, tq=tq, tkc=tkc, qb=qb)
    return pl.pallas_call(
        kern,
        grid=(B, H // _HP, S // (tq * qb)),
        in_specs=[
            pl.BlockSpec((1, _HP, tq * qb, LANES), lambda b, h, q: (b, h, q, 0)),
            pl.BlockSpec((1, _HP, n, tkc, LANES), lambda b, h, q: (b, h, 0, 0, 0)),
            pl.BlockSpec((1, n, _HP * FX_DH, tkc), lambda b, h, q: (b, 0, h, 0)),
        ],
        out_specs=pl.BlockSpec((1, tq * qb, _HP * FX_DH), lambda b, h, q: (b, q, h)),
        out_shape=jax.ShapeDtypeStruct((B, S, FX_W), BF16),
        scratch_shapes=[
            pltpu.VMEM((_HP, 1, tq), F32),
            pltpu.VMEM((_HP, _VROWS, tq), F32),
            pltpu.VMEM((tkc // tq * _HP, tq, tq), BF16),
        ],
        compiler_params=pltpu.CompilerParams(
            dimension_semantics=("arbitrary", "arbitrary", "arbitrary"),
            vmem_limit_bytes=VMEM_LIMIT),
        name="fox",
    )(fq, fk5, fvt)


def _memkv_kernel(mem_ref, g_ref, w_ref, k_ref, v_ref):
    mn = _rms(mem_ref[0], g_ref[...]).astype(BF16)
    k_ref[0] = jnp.dot(mn, w_ref[:, 0:D_MODEL], preferred_element_type=F32).astype(BF16)
    v_ref[0] = jnp.dot(mn, w_ref[:, D_MODEL:2 * D_MODEL], preferred_element_type=F32).astype(BF16)


def _memkv(mem, ln_mem, w_xkv):
    B, M, _ = mem.shape
    spec = pl.BlockSpec((1, M, D_MODEL), lambda b: (b, 0, 0))
    return pl.pallas_call(
        _memkv_kernel,
        grid=(B,),
        in_specs=[spec, _const_spec((1, D_MODEL)), _const_spec(w_xkv.shape)],
        out_specs=(spec, spec),
        out_shape=(jax.ShapeDtypeStruct((B, M, D_MODEL), BF16),) * 2,
        compiler_params=pltpu.CompilerParams(
            dimension_semantics=("arbitrary",), vmem_limit_bytes=VMEM_LIMIT),
        name="memkv",
    )(mem, ln_mem, w_xkv)


_ROW_GROUPS = 2


def _xattn_kernel(x_ref, ml_ref, fx_ref, wout_ref, lnx_ref, wq_ref, km_ref, vm_ref, wo_ref,
                  out_ref):
    tm = x_ref.shape[1]
    groups = [slice(r, r + tm // _ROW_GROUPS) for r in range(0, tm, tm // _ROW_GROUPS)]
    sls = [slice(hd * X_DH, (hd + 1) * X_DH) for hd in range(X_HEADS)]
    km = km_ref[0]
    vm = vm_ref[0]
    x1 = [x_ref[0, g, :]
          + jnp.dot(ml_ref[0, g, :], wout_ref[0:ML_W, :], preferred_element_type=F32)
          + jnp.dot(fx_ref[0, g, :], wout_ref[ML_W:ML_W + FX_W, :], preferred_element_type=F32)
          for g in groups]
    q = [(jnp.dot(_rms(x, lnx_ref[...]).astype(BF16), wq_ref[...], preferred_element_type=F32)
          * (X_DH ** -0.5)).astype(BF16) for x in x1]
    s = [[lax.dot_general(qg[:, sl], km[:, sl], (((1,), (1,)), ((), ())),
                          preferred_element_type=F32) for sl in sls] for qg in q]
    o = []
    for sg in s:
        heads = []
        for hd, sl in enumerate(sls):
            p = jnp.exp(sg[hd] - jnp.max(sg[hd], axis=-1, keepdims=True))
            l = jnp.sum(p, axis=-1, keepdims=True)
            oh = jnp.dot(p.astype(BF16), vm[:, sl], preferred_element_type=F32) / l
            heads.append(oh.astype(BF16))
        o.append(jnp.concatenate(heads, axis=1))
    for g, xg, og in zip(groups, x1, o):
        out_ref[0, g, :] = xg + jnp.dot(og, wo_ref[...], preferred_element_type=F32)


def _xattn(x, ml, fx, w_out, ln_x, w_xq, kmem, vmem, w_xo, *, tm):
    B, S, _ = x.shape
    M = kmem.shape[1]
    row_spec = lambda n: pl.BlockSpec((1, tm, n), lambda b, s: (b, s, 0))
    mem_spec = pl.BlockSpec((1, M, D_MODEL), lambda b, s: (b, 0, 0))
    return pl.pallas_call(
        _xattn_kernel,
        grid=(B, S // tm),
        in_specs=[
            row_spec(D_MODEL), row_spec(ML_W), row_spec(FX_W),
            _const_spec(w_out.shape), _const_spec((1, D_MODEL)), _const_spec(w_xq.shape),
            mem_spec, mem_spec, _const_spec(w_xo.shape),
        ],
        out_specs=row_spec(D_MODEL),
        out_shape=jax.ShapeDtypeStruct((B, S, D_MODEL), F32),
        compiler_params=pltpu.CompilerParams(
            dimension_semantics=("arbitrary", "arbitrary"), vmem_limit_bytes=VMEM_LIMIT),
        name="xattn",
    )(x, ml, fx, w_out, ln_x, w_xq, kmem, vmem, w_xo)


_FF_CHUNK = 1024


def _mlp_kernel(x_ref, ln2_ref, w1_ref, w2_ref, lnf_ref, out_ref):
    x = x_ref[0]
    h = _rms(x, ln2_ref[...]).astype(BF16)
    acc = x
    for c in range(D_FF // _FF_CHUNK):
        sl = slice(c * _FF_CHUNK, (c + 1) * _FF_CHUNK)
        u = jnp.maximum(jnp.dot(h, w1_ref[:, sl], preferred_element_type=F32), 0.0)
        acc = acc + jnp.dot((u * u).astype(BF16), w2_ref[sl, :], preferred_element_type=F32)
    out_ref[0] = _rms(acc, lnf_ref[...])


def _mlp(x, ln2, w1, w2, ln_f, *, tm):
    B, S, _ = x.shape
    row_spec = pl.BlockSpec((1, tm, D_MODEL), lambda b, s: (b, s, 0))
    return pl.pallas_call(
        _mlp_kernel,
        grid=(B, S // tm),
        in_specs=[row_spec, _const_spec((1, D_MODEL)), _const_spec(w1.shape),
                  _const_spec(w2.shape), _const_spec((1, D_MODEL))],
        out_specs=row_spec,
        out_shape=jax.ShapeDtypeStruct((B, S, D_MODEL), F32),
        compiler_params=pltpu.CompilerParams(
            dimension_semantics=("arbitrary", "arbitrary"), vmem_limit_bytes=VMEM_LIMIT),
        name="mlp",
    )(x, ln2, w1, w2, ln_f)


def _layer(x, mem, ln1, w_in, conv_w, conv_b, b_i, b_f, ml_norm, fx_b_f, w_out,
           ln_x, ln_mem, w_xq, w_xkv, w_xo, ln2, w_ff1, w_ff2, ln_f_or_none):
    B, S, _ = x.shape
    t = _tiles(S)
    o_qk, o_v, o_o = 0, 2 * ML_W, 3 * ML_W
    o_i = 4 * ML_W
    o_f = o_i + ML_HEADS
    o_fq = o_f + ML_HEADS
    o_fk, o_fv = o_fq + FX_W, o_fq + 2 * FX_W
    o_ff = o_fq + 3 * FX_W
    w_gate = jnp.concatenate(
        [w_in[:, o_i:o_fq], w_in[:, o_ff:o_ff + FX_HEADS]], axis=1)
    w_main = jnp.concatenate(
        [w_in[:, o_qk:o_i], w_in[:, o_fq:o_fv],
         jnp.pad(w_gate, ((0, 0), (0, LANES - _N_GT)))], axis=1).astype(BF16)
    w_t = jnp.concatenate([w_in[:, o_fv:o_ff], w_gate], axis=1).T.astype(BF16)
    bias = jnp.concatenate([b_i, b_f, fx_b_f]).astype(F32)
    brow = jnp.pad(bias, (0, LANES - _N_GT)).reshape(1, LANES)
    bcol = bias.reshape(_N_GT, 1)

    mq, mk, v, o, gate, gatet, fq, fk, fvt = _proj(
        x, ln1.reshape(1, -1), w_main, w_t, brow, bcol, conv_w, conv_b.reshape(1, -1),
        tm=t["tm"], tkc=t["tkc"], L=t["chunk"])
    ml = _mlstm(mq, mk, v, o, gate, gatet, ml_norm.reshape(1, -1), L=t["chunk"],
                NB=max(n for n in range(1, t["ml_rows"] + 1) if B % n == 0))
    fx = _fox(fq, fk, fvt, tq=t["tq"], tkc=t["tkc"], qb=t["qb"])
    kmem, vmem = _memkv(mem, ln_mem.reshape(1, -1), w_xkv.astype(BF16))
    x2 = _xattn(x, ml, fx, w_out.astype(BF16), ln_x.reshape(1, -1), w_xq.astype(BF16),
                kmem, vmem, w_xo.astype(BF16), tm=t["tail"])
    return _mlp(x2, ln2.reshape(1, -1), w_ff1.astype(BF16), w_ff2.astype(BF16),
                ln_f_or_none.reshape(1, -1), tm=t["tail"])


def kernel(x, mem, ln1, w_in, ml_conv_w, ml_conv_b, ml_b_i, ml_b_f, ml_norm, fx_b_f, w_out,
           ln_x, ln_mem, w_xq, w_xkv, w_xo, ln2, w_ff1, w_ff2, ln_f):
    depth = w_in.shape[0]
    assert depth == 1, "single-layer problem: the final norm is fused into the layer's MLP kernel"
    return _layer(x, mem, ln1[0], w_in[0], ml_conv_w[0], ml_conv_b[0], ml_b_i[0], ml_b_f[0],
                  ml_norm[0], fx_b_f[0], w_out[0], ln_x[0], ln_mem[0], w_xq[0], w_xkv[0],
                  w_xo[0], ln2[0], w_ff1[0], w_ff2[0], ln_f)
```

```python
import functools

import jax
import jax.numpy as jnp
from jax import lax
from jax.experimental import pallas as pl
from jax.experimental.pallas import tpu as pltpu

F32 = jnp.float32
BF16 = jnp.bfloat16

D_MODEL = 1024
ML_HEADS = 4
ML_DH = 128
ML_W = ML_HEADS * ML_DH
CONV_W = 4
FX_HEADS = 8
FX_DH = 64
FX_W = FX_HEADS * FX_DH
X_HEADS = 4
X_DH = D_MODEL // X_HEADS
D_FF = 4 * D_MODEL
EPS = 1e-6

LANES = 128
NEG_BIG = -1e30
LOG2E = 1.4426950408889634
_TAIL = 8
VMEM_LIMIT = 56 * 1024 * 1024

_C_QK = 0
_C_V = _C_QK + 2 * ML_W
_C_O = _C_V + ML_W
_C_FQ = _C_O + ML_W
_C_FK = _C_FQ + FX_W
_C_G = _C_FK + FX_W
_C_END = _C_G + LANES
_R_GT = FX_W
_N_GT = 16

_AUG0 = FX_DH


def _tiles(S):
    tm = min(512, S)
    return dict(
        tm=tm,
        chunk=min(256, S),
        ml_rows=4,
        tq=min(256, S),
        ts=min(256, S),
        qb=min(8, S // min(256, S)),
        tkc=max(tm, min(4096, S)),
        tail=min(1024, S),
    )


def _rms(x, g):
    return x * lax.rsqrt(jnp.mean(x * x, axis=-1, keepdims=True) + EPS) * g


def _split3(x):
    hi = x.astype(BF16)
    r = x - hi.astype(F32)
    mid = r.astype(BF16)
    lo = (r - mid.astype(F32)).astype(BF16)
    return hi, mid, lo


def _log_sigmoid(x):
    return jnp.minimum(x, 0.0) - jnp.log1p(jnp.exp(-jnp.abs(x)))


def _const_spec(shape):
    nd = len(shape)
    return pl.BlockSpec(shape, lambda *_: (0,) * nd, pipeline_mode=pl.Buffered(1))


def _proj_kernel(x_ref, g_ref, w_ref, wt_ref, brow_ref, bcol_ref, cw_ref, cb_ref,
                 mq_ref, mk_ref, v_ref, o_ref, gate_ref, gatet_ref, fq_ref, fk_ref, fvt_ref,
                 carry_ref, zbuf_ref, *, tm, L):
    @pl.when(pl.program_id(1) == 0)
    def _():
        carry_ref[...] = jnp.zeros_like(carry_ref)
        zbuf_ref[0:_TAIL, :] = jnp.zeros((_TAIL, 2 * ML_W), F32)

    x = x_ref[0]
    h = _rms(x, g_ref[...]).astype(BF16)

    def proj(c0, c1):
        return jnp.dot(h, w_ref[:, c0:c1], preferred_element_type=F32)

    lane = lax.broadcasted_iota(jnp.int32, (tm, LANES), 1)

    def in_lanes(lo, n):
        return (lane >= lo) & (lane < lo + n)

    gpre = proj(_C_G, _C_END) + brow_ref[...]
    zq = proj(_C_FQ, _C_FK) * (FX_DH ** -0.5 * LOG2E)
    zk = proj(_C_FK, _C_G)
    gls = _log_sigmoid(gpre)
    tril = (lax.broadcasted_iota(jnp.int32, (L, L), 1)
            <= lax.broadcasted_iota(jnp.int32, (L, L), 0)).astype(BF16)
    is_ig = lax.broadcasted_iota(jnp.int32, (L, LANES), 1) < ML_HEADS
    parts = _split3(gls)
    carry = carry_ref[...]
    csum_chunks = []
    for ci in range(tm // L):
        rows = slice(ci * L, (ci + 1) * L)
        local = sum(jnp.dot(tril, part[rows], preferred_element_type=F32) for part in parts)
        gate_ref[0, rows, :] = LOG2E * jnp.where(is_ig, gpre[rows], local)
        csum_chunks.append(carry + local)
        carry = carry + local[L - 1:L, :]
    carry_ref[...] = carry
    csum = jnp.concatenate(csum_chunks, axis=0)

    c_hi, c_mid, c_lo = (p.astype(F32) for p in _split3(csum * LOG2E))
    src = 2 * ML_HEADS
    cq = jnp.where(in_lanes(_AUG0, 8), pltpu.roll(c_hi, _AUG0 - src, 1),
         jnp.where(in_lanes(_AUG0 + 8, 8), pltpu.roll(c_mid, _AUG0 + 8 - src, 1),
         jnp.where(in_lanes(_AUG0 + 16, 8), pltpu.roll(c_lo, _AUG0 + 16 - src, 1), 0.0)))
    ck = -pltpu.roll(cq, 24, 1)
    lane_row = lax.broadcasted_iota(jnp.int32, (1, LANES), 1)
    for hd in range(FX_HEADS):
        grp = hd // 2
        zq_h = zq[:, grp * LANES:(grp + 1) * LANES]
        zk_h = zk[:, grp * LANES:(grp + 1) * LANES]
        if hd % 2:
            zq_h = pltpu.roll(zq_h, LANES // 2, 1)
            zk_h = pltpu.roll(zk_h, LANES // 2, 1)
        hot = [(lane_row == base0 + 8 * j + hd) for j in range(3) for base0 in (_AUG0, _AUG0 + 24)]
        hot_k = (hot[0] | hot[2] | hot[4]).astype(F32)
        hot_q = (hot[1] | hot[3] | hot[5]).astype(F32)
        fq_ref[0, hd] = jnp.where(lane < _AUG0, zq_h, cq + hot_q).astype(BF16)
        fk_ref[0, hd] = jnp.where(lane < _AUG0, zk_h, ck + hot_k).astype(BF16)

    zqk = proj(_C_QK, _C_V)
    zbuf_ref[_TAIL:_TAIL + tm, :] = zqk
    conv = cb_ref[...] + zqk * cw_ref[CONV_W - 1:CONV_W, :]
    for j in range(CONV_W - 1):
        off = _TAIL - (CONV_W - 1) + j
        conv = conv + zbuf_ref[off:off + tm, :] * cw_ref[j:j + 1, :]
    zbuf_ref[0:_TAIL, :] = zbuf_ref[tm:tm + _TAIL, :]
    act = conv * jax.nn.sigmoid(conv)
    mq_ref[0] = act[:, 0:ML_W].astype(BF16)
    mk_ref[0] = (act[:, ML_W:2 * ML_W] * (ML_DH ** -0.5)).astype(BF16)

    zt = lax.dot_general(wt_ref[...], h, (((1,), (1,)), ((), ())), preferred_element_type=F32)
    fvt_ref[0, 0] = zt[0:FX_W, :].astype(BF16)
    gt = zt[_R_GT:_R_GT + _N_GT, :] + bcol_ref[...]
    glt = _log_sigmoid(gt)
    row = lax.broadcasted_iota(jnp.int32, (_N_GT, L), 0)
    triu = (lax.broadcasted_iota(jnp.int32, (L, L), 0)
            <= lax.broadcasted_iota(jnp.int32, (L, L), 1)).astype(BF16)
    for ci in range(tm // L):
        cols = slice(ci * L, (ci + 1) * L)
        brow = sum(jnp.dot(part[:, cols], triu, preferred_element_type=F32)
                   for part in _split3(glt))
        gatet_ref[0, :, cols] = LOG2E * jnp.where(row < ML_HEADS, gt[:, cols], brow)

    v_ref[0] = proj(_C_V, _C_O).astype(BF16)
    o_ref[0] = proj(_C_O, _C_FQ).astype(BF16)


def _proj(x, ln1, w_main, w_t, brow, bcol, conv_w, conv_b, *, tm, tkc, L):
    B, S, _ = x.shape
    ns = S // tm
    r = tkc // tm
    kern = functools.partial(_proj_kernel, tm=tm, L=L)
    row_spec = lambda n: pl.BlockSpec((1, tm, n), lambda b, s: (b, s, 0))
    out_shape = (
        jax.ShapeDtypeStruct((B, S, ML_W), BF16),
        jax.ShapeDtypeStruct((B, S, ML_W), BF16),
        jax.ShapeDtypeStruct((B, S, ML_W), BF16),
        jax.ShapeDtypeStruct((B, S, ML_W), BF16),
        jax.ShapeDtypeStruct((B, S, LANES), F32),
        jax.ShapeDtypeStruct((B, _N_GT, S), F32),
        jax.ShapeDtypeStruct((B, FX_HEADS, S, LANES), BF16),
        jax.ShapeDtypeStruct((B, FX_HEADS, S, LANES), BF16),
        jax.ShapeDtypeStruct((B, S // tkc, FX_W, tkc), BF16),
    )
    out_specs = (
        row_spec(ML_W), row_spec(ML_W), row_spec(ML_W), row_spec(ML_W), row_spec(LANES),
        pl.BlockSpec((1, _N_GT, tm), lambda b, s: (b, 0, s)),
        pl.BlockSpec((1, FX_HEADS, tm, LANES), lambda b, s: (b, 0, s, 0)),
        pl.BlockSpec((1, FX_HEADS, tm, LANES), lambda b, s: (b, 0, s, 0)),
        pl.BlockSpec((1, 1, FX_W, tm), lambda b, s: (b, s // r, 0, s % r)),
    )
    return pl.pallas_call(
        kern,
        grid=(B, ns),
        in_specs=[
            row_spec(D_MODEL),
            _const_spec((1, D_MODEL)),
            _const_spec(w_main.shape),
            _const_spec(w_t.shape),
            _const_spec((1, LANES)),
            _const_spec((_N_GT, 1)),
            _const_spec((CONV_W, 2 * ML_W)),
            _const_spec((1, 2 * ML_W)),
        ],
        out_specs=out_specs,
        out_shape=out_shape,
        scratch_shapes=[pltpu.VMEM((1, LANES), F32),
                        pltpu.VMEM((tm + _TAIL, 2 * ML_W), F32)],
        compiler_params=pltpu.CompilerParams(
            dimension_semantics=("arbitrary", "arbitrary"), vmem_limit_bytes=VMEM_LIMIT),
        name="proj",
    )(x, ln1, w_main, w_t, brow, bcol, conv_w, conv_b)


_ML_SL = [slice(hd * ML_DH, (hd + 1) * ML_DH) for hd in range(ML_HEADS)]
_V_INTER, _V_EM, _V_WA, _V_DECAY = 0, 1, 2, 3


def _mlstm_gates(gate_ref, gatet_ref, m_ref, et_ref, vec_ref, slot, *, L, NB):
    src_i = lax.broadcasted_iota(jnp.int32, (L, L), 0)
    tgt_i = lax.broadcasted_iota(jnp.int32, (L, L), 1)
    causal_t = src_i <= tgt_i
    gates = [(gate_ref[nb], gatet_ref[nb]) for nb in range(NB)]
    for ch in range(NB * ML_HEADS):
        nb, hd = divmod(ch, ML_HEADS)
        gate, gatet = gates[nb]
        ig_row = gatet[hd:hd + 1, :]
        b_row = gatet[ML_HEADS + hd:ML_HEADS + hd + 1, :]
        r_col = gate[:, hd:hd + 1] - gate[:, ML_HEADS + hd:ML_HEADS + hd + 1]
        g = b_row[:, L - 1:L]
        m_prev = m_ref[ch:ch + 1, 0:1]
        dmat_t = jnp.where(causal_t, b_row + r_col, NEG_BIG)
        m_inter = b_row + m_prev
        m_t = jnp.maximum(m_inter, jnp.max(dmat_t, axis=0, keepdims=True))
        a_row = g - b_row + ig_row
        m_new = jnp.maximum(g + m_prev, jnp.max(a_row, axis=1, keepdims=True))
        m_ref[ch:ch + 1, :] = jnp.broadcast_to(m_new, (1, LANES))
        et_ref[slot, ch] = jnp.exp2(dmat_t - m_t)
        vec_ref[slot, ch, _V_INTER:_V_INTER + 1, :] = jnp.exp2(m_inter - m_t)
        vec_ref[slot, ch, _V_EM:_V_EM + 1, :] = jnp.exp2(-m_t)
        vec_ref[slot, ch, _V_WA:_V_WA + 1, :] = jnp.exp2(a_row - m_new)
        vec_ref[slot, ch, _V_DECAY:_V_DECAY + 1, :] = jnp.broadcast_to(
            jnp.exp2(g + m_prev - m_new), (1, L))


def _mlstm_mix(q_ref, k_ref, v_ref, et_ref, vec_ref, state_ref, hvt_ref, slot, *, L, NB):
    one_rows = (lax.broadcasted_iota(jnp.int32, (ML_DH, L), 0) == 0).astype(BF16)
    heads = range(NB * ML_HEADS)
    rows = lambda ref, ch: ref[ch // ML_HEADS, :, _ML_SL[ch % ML_HEADS]]
    vec = [vec_ref[slot, hd] for hd in heads]
    q = [rows(q_ref, hd) for hd in heads]
    k = [rows(k_ref, hd) for hd in heads]
    vaug_t = [jnp.concatenate([rows(v_ref, hd).T, one_rows], axis=0)
              for hd in heads]
    vaug_tw = [(vaug_t[hd].astype(F32) * vec[hd][_V_WA:_V_WA + 1, :]).astype(BF16) for hd in heads]
    s_t = [lax.dot_general(k[hd], q[hd], (((1,), (1,)), ((), ())),
                           preferred_element_type=F32) for hd in heads]
    qs_t = [lax.dot_general(state_ref[hd].astype(BF16), q[hd], (((1,), (1,)), ((), ())),
                            preferred_element_type=F32) for hd in heads]
    upd = [jnp.dot(vaug_tw[hd], k[hd], preferred_element_type=F32) for hd in heads]
    for hd in heads:
        state_ref[hd] = vec[hd][_V_DECAY:_V_DECAY + 1, 0:1] * state_ref[hd] + upd[hd]
    scores_t = [(s_t[hd] * et_ref[slot, hd]).astype(BF16) for hd in heads]
    nd_t = [jnp.dot(vaug_t[hd], scores_t[hd], preferred_element_type=F32)
            + vec[hd][_V_INTER:_V_INTER + 1, :] * qs_t[hd] for hd in heads]
    for hd in heads:
        den = nd_t[hd][ML_DH:ML_DH + 1, :]
        hvt_ref[slot, hd] = nd_t[hd][0:ML_DH, :] / jnp.maximum(
            jnp.abs(den), vec[hd][_V_EM:_V_EM + 1, :])


def _mlstm_out(hvt_ref, o_ref, nrm_ref, out_ref, slot, *, NB):
    for ch in range(NB * ML_HEADS):
        nb, hd = divmod(ch, ML_HEADS)
        hv_t = hvt_ref[slot, ch]
        hn_t = hv_t * lax.rsqrt(jnp.mean(hv_t * hv_t, axis=0, keepdims=True) + EPS)
        og = jax.nn.sigmoid(o_ref[nb, :, _ML_SL[hd]].astype(F32))
        out_ref[nb, :, _ML_SL[hd]] = (hn_t.T * nrm_ref[:, _ML_SL[hd]] * og).astype(BF16)


def _mlstm_kernel(q_ref, k_ref, v_ref, o_ref, gate0_ref, gatet0_ref, gate_ref, gatet_ref, nrm_ref,
                  out_ref, state_ref, m_ref, e_ref, vec_ref, hv_ref, *, L, NB):
    c = pl.program_id(1)

    @pl.when(c == 0)
    def _():
        state_ref[...] = jnp.zeros_like(state_ref)
        m_ref[...] = jnp.zeros_like(m_ref)
        hv_ref[...] = jnp.zeros_like(hv_ref)
        _mlstm_gates(gate0_ref, gatet0_ref, m_ref, e_ref, vec_ref, 0, L=L, NB=NB)

    slot = c % 2
    _mlstm_out(hv_ref, o_ref, nrm_ref, out_ref, 1 - slot, NB=NB)
    _mlstm_mix(q_ref, k_ref, v_ref, e_ref, vec_ref, state_ref, hv_ref, slot, L=L, NB=NB)
    _mlstm_gates(gate_ref, gatet_ref, m_ref, e_ref, vec_ref, 1 - slot, L=L, NB=NB)


def _mlstm(q, k, v, o, gate, gatet, nrm, *, L, NB):
    B, S, _ = q.shape
    nc = S // L
    nch = NB * ML_HEADS
    kern = functools.partial(_mlstm_kernel, L=L, NB=NB)
    cur = lambda b, c: (b, jnp.minimum(c, nc - 1), 0)
    prev = lambda b, c: (b, jnp.maximum(c - 1, 0), 0)
    row_spec = lambda n, im: pl.BlockSpec((NB, L, n), im)
    return pl.pallas_call(
        kern,
        grid=(B // NB, nc + 1),
        in_specs=[
            row_spec(ML_W, cur), row_spec(ML_W, cur), row_spec(ML_W, cur), row_spec(ML_W, prev),
            row_spec(LANES, lambda b, c: (b, 0, 0)),
            pl.BlockSpec((NB, _N_GT, L), lambda b, c: (b, 0, 0)),
            row_spec(LANES, lambda b, c: (b, jnp.minimum(c + 1, nc - 1), 0)),
            pl.BlockSpec((NB, _N_GT, L), lambda b, c: (b, 0, jnp.minimum(c + 1, nc - 1))),
            _const_spec((1, ML_W)),
        ],
        out_specs=row_spec(ML_W, prev),
        out_shape=jax.ShapeDtypeStruct((B, S, ML_W), BF16),
        scratch_shapes=[
            pltpu.VMEM((nch, 2 * ML_DH, ML_DH), F32),
            pltpu.VMEM((-(-nch // 8) * 8, LANES), F32),
            pltpu.VMEM((2, nch, L, L), F32),
            pltpu.VMEM((2, nch, 4, L), F32),
            pltpu.VMEM((2, nch, ML_DH, L), F32),
        ],
        compiler_params=pltpu.CompilerParams(
            dimension_semantics=("arbitrary", "arbitrary"), vmem_limit_bytes=VMEM_LIMIT),
        name="mlstm",
    )(q, k, v, o, gate, gatet, gate, gatet, nrm)


_HP = 4
_VROWS = FX_DH + 16
_LAG = 3 * _HP


def _fox_kernel(q_ref, k_ref, vt_ref, o_ref, m_ref, acc_ref, p_ref, *, tq, ts, tkc, qb):
    nsub = tkc // ts
    per = ts // tq
    key_i = lax.broadcasted_iota(jnp.int32, (ts, tq), 0)
    qry_i = lax.broadcasted_iota(jnp.int32, (ts, tq), 1)
    ones_rows = (lax.broadcasted_iota(jnp.int32, (_VROWS - FX_DH, ts), 0) == 0).astype(BF16)

    def query_block(qq, _):
        qi = pl.program_id(2) * qb + qq
        q_rows = pl.ds(pl.multiple_of(qq * tq, tq), tq)
        n_full = qi // (nsub * per)
        rem = qi % (nsub * per)

        m_ref[...] = jnp.full_like(m_ref, NEG_BIG)
        acc_ref[...] = jnp.zeros_like(acc_ref)

        def chunk(j, n_plain, diagonal, q_off=0):
            state = [(m_ref[hd], acc_ref[hd]) for hd in range(_HP)]
            tiles = [(i, hd) for i in range(n_plain + int(diagonal)) for hd in range(_HP)]
            scores = {}

            def qk(i, hd):
                st = lax.dot_general(k_ref[0, hd, j, i * ts:(i + 1) * ts, :],
                                     q_ref[0, hd, q_rows, :], (((1,), (1,)), ((), ())),
                                     preferred_element_type=F32)
                if diagonal and i == n_plain:
                    st = jnp.where(qry_i + q_off >= key_i, st, NEG_BIG)
                scores[i, hd] = st

            alphas = {}

            def softmax(t, i, hd):
                m, acc = state[hd]
                st = scores.pop((i, hd))
                m_new = jnp.maximum(m, jnp.max(st, axis=0, keepdims=True))
                alphas[i, hd] = jnp.exp2(m - m_new)
                p_ref[t] = jnp.exp2(st - m_new).astype(BF16)
                state[hd] = (m_new, acc)

            def pv(t, i, hd):
                m, acc = state[hd]
                vt = jnp.concatenate(
                    [vt_ref[0, j, hd * FX_DH:(hd + 1) * FX_DH, i * ts:(i + 1) * ts], ones_rows],
                    axis=0)
                acc = alphas.pop((i, hd)) * acc + jnp.dot(
                    vt, p_ref[t], preferred_element_type=F32)
                state[hd] = (m, acc)

            for t in range(len(tiles) + _LAG):
                if t < len(tiles):
                    qk(*tiles[t])
                if _HP <= t < len(tiles) + _HP:
                    softmax(t - _HP, *tiles[t - _HP])
                if t >= _LAG:
                    pv(t - _LAG, *tiles[t - _LAG])
            for hd in range(_HP):
                m_ref[hd], acc_ref[hd] = state[hd]

        def body(j, _):
            chunk(j, nsub, False)
            return 0

        lax.fori_loop(0, n_full, body, 0)
        for r in range(nsub * per):
            pl.when(rem == r)(functools.partial(chunk, n_full, r // per, True, (r % per) * tq))
        out_t = jnp.concatenate(
            [acc_ref[hd, 0:FX_DH, :] / acc_ref[hd, FX_DH:FX_DH + 1, :] for hd in range(_HP)],
            axis=0)
        o_ref[0, q_rows, :] = out_t.T.astype(BF16)
        return 0

    lax.fori_loop(0, qb, query_block, 0)


def _fox(fq, fk, fvt, *, tq, ts, tkc, qb):
    B, H, S, _ = fq.shape
    n = S // tkc
    fk5 = fk.reshape(B, H, n, tkc, LANES)
    kern = functools.partial(_fox_kernel, tq=tq, ts=ts, tkc=tkc, qb=qb)
    return pl.pallas_call(
        kern,
        grid=(B, H // _HP, S // (tq * qb)),
        in_specs=[
            pl.BlockSpec((1, _HP, tq * qb, LANES), lambda b, h, q: (b, h, q, 0)),
            pl.BlockSpec((1, _HP, n, tkc, LANES), lambda b, h, q: (b, h, 0, 0, 0)),
            pl.BlockSpec((1, n, _HP * FX_DH, tkc), lambda b, h, q: (b, 0, h, 0)),
        ],
        out_specs=pl.BlockSpec((1, tq * qb, _HP * FX_DH), lambda b, h, q: (b, q, h)),
        out_shape=jax.ShapeDtypeStruct((B, S, FX_W), BF16),
        scratch_shapes=[
            pltpu.VMEM((_HP, 1, tq), F32),
            pltpu.VMEM((_HP, _VROWS, tq), F32),
            pltpu.VMEM((tkc // ts * _HP, ts, tq), BF16),
        ],
        compiler_params=pltpu.CompilerParams(
            dimension_semantics=("arbitrary", "arbitrary", "arbitrary"),
            vmem_limit_bytes=VMEM_LIMIT),
        name="fox",
    )(fq, fk5, fvt)


def _memkv_kernel(mem_ref, g_ref, w_ref, k_ref, v_ref):
    mn = _rms(mem_ref[0], g_ref[...]).astype(BF16)
    k_ref[0] = jnp.dot(mn, w_ref[:, 0:D_MODEL], preferred_element_type=F32).astype(BF16)
    v_ref[0] = jnp.dot(mn, w_ref[:, D_MODEL:2 * D_MODEL], preferred_element_type=F32).astype(BF16)


def _memkv(mem, ln_mem, w_xkv):
    B, M, _ = mem.shape
    spec = pl.BlockSpec((1, M, D_MODEL), lambda b: (b, 0, 0))
    return pl.pallas_call(
        _memkv_kernel,
        grid=(B,),
        in_specs=[spec, _const_spec((1, D_MODEL)), _const_spec(w_xkv.shape)],
        out_specs=(spec, spec),
        out_shape=(jax.ShapeDtypeStruct((B, M, D_MODEL), BF16),) * 2,
        compiler_params=pltpu.CompilerParams(
            dimension_semantics=("arbitrary",), vmem_limit_bytes=VMEM_LIMIT),
        name="memkv",
    )(mem, ln_mem, w_xkv)


_ROW_GROUPS = 2


def _xattn_kernel(x_ref, ml_ref, fx_ref, wout_ref, lnx_ref, wq_ref, km_ref, vm_ref, wo_ref,
                  out_ref):
    tm = x_ref.shape[1]
    groups = [slice(r, r + tm // _ROW_GROUPS) for r in range(0, tm, tm // _ROW_GROUPS)]
    sls = [slice(hd * X_DH, (hd + 1) * X_DH) for hd in range(X_HEADS)]
    km = km_ref[0]
    vm = vm_ref[0]
    x1 = [x_ref[0, g, :]
          + jnp.dot(ml_ref[0, g, :], wout_ref[0:ML_W, :], preferred_element_type=F32)
          + jnp.dot(fx_ref[0, g, :], wout_ref[ML_W:ML_W + FX_W, :], preferred_element_type=F32)
          for g in groups]
    q = [(jnp.dot(_rms(x, lnx_ref[...]).astype(BF16), wq_ref[...], preferred_element_type=F32)
          * (X_DH ** -0.5)).astype(BF16) for x in x1]
    s = [[lax.dot_general(qg[:, sl], km[:, sl], (((1,), (1,)), ((), ())),
                          preferred_element_type=F32) for sl in sls] for qg in q]
    o = []
    for sg in s:
        heads = []
        for hd, sl in enumerate(sls):
            p = jnp.exp(sg[hd] - jnp.max(sg[hd], axis=-1, keepdims=True))
            l = jnp.sum(p, axis=-1, keepdims=True)
            oh = jnp.dot(p.astype(BF16), vm[:, sl], preferred_element_type=F32) / l
            heads.append(oh.astype(BF16))
        o.append(jnp.concatenate(heads, axis=1))
    for g, xg, og in zip(groups, x1, o):
        out_ref[0, g, :] = xg + jnp.dot(og, wo_ref[...], preferred_element_type=F32)


def _xattn(x, ml, fx, w_out, ln_x, w_xq, kmem, vmem, w_xo, *, tm):
    B, S, _ = x.shape
    M = kmem.shape[1]
    row_spec = lambda n: pl.BlockSpec((1, tm, n), lambda b, s: (b, s, 0))
    mem_spec = pl.BlockSpec((1, M, D_MODEL), lambda b, s: (b, 0, 0))
    return pl.pallas_call(
        _xattn_kernel,
        grid=(B, S // tm),
        in_specs=[
            row_spec(D_MODEL), row_spec(ML_W), row_spec(FX_W),
            _const_spec(w_out.shape), _const_spec((1, D_MODEL)), _const_spec(w_xq.shape),
            mem_spec, mem_spec, _const_spec(w_xo.shape),
        ],
        out_specs=row_spec(D_MODEL),
        out_shape=jax.ShapeDtypeStruct((B, S, D_MODEL), F32),
        compiler_params=pltpu.CompilerParams(
            dimension_semantics=("arbitrary", "arbitrary"), vmem_limit_bytes=VMEM_LIMIT),
        name="xattn",
    )(x, ml, fx, w_out, ln_x, w_xq, kmem, vmem, w_xo)


_FF_CHUNK = 1024


def _mlp_kernel(x_ref, ln2_ref, w1_ref, w2_ref, lnf_ref, out_ref):
    x = x_ref[0]
    h = _rms(x, ln2_ref[...]).astype(BF16)
    acc = x
    for c in range(D_FF // _FF_CHUNK):
        sl = slice(c * _FF_CHUNK, (c + 1) * _FF_CHUNK)
        u = jnp.maximum(jnp.dot(h, w1_ref[:, sl], preferred_element_type=F32), 0.0)
        acc = acc + jnp.dot((u * u).astype(BF16), w2_ref[sl, :], preferred_element_type=F32)
    out_ref[0] = _rms(acc, lnf_ref[...])


def _mlp(x, ln2, w1, w2, ln_f, *, tm):
    B, S, _ = x.shape
    row_spec = pl.BlockSpec((1, tm, D_MODEL), lambda b, s: (b, s, 0))
    return pl.pallas_call(
        _mlp_kernel,
        grid=(B, S // tm),
        in_specs=[row_spec, _const_spec((1, D_MODEL)), _const_spec(w1.shape),
                  _const_spec(w2.shape), _const_spec((1, D_MODEL))],
        out_specs=row_spec,
        out_shape=jax.ShapeDtypeStruct((B, S, D_MODEL), F32),
        compiler_params=pltpu.CompilerParams(
            dimension_semantics=("arbitrary", "arbitrary"), vmem_limit_bytes=VMEM_LIMIT),
        name="mlp",
    )(x, ln2, w1, w2, ln_f)


def _layer(x, mem, ln1, w_in, conv_w, conv_b, b_i, b_f, ml_norm, fx_b_f, w_out,
           ln_x, ln_mem, w_xq, w_xkv, w_xo, ln2, w_ff1, w_ff2, ln_f_or_none):
    B, S, _ = x.shape
    t = _tiles(S)
    o_qk, o_v, o_o = 0, 2 * ML_W, 3 * ML_W
    o_i = 4 * ML_W
    o_f = o_i + ML_HEADS
    o_fq = o_f + ML_HEADS
    o_fk, o_fv = o_fq + FX_W, o_fq + 2 * FX_W
    o_ff = o_fq + 3 * FX_W
    w_gate = jnp.concatenate(
        [w_in[:, o_i:o_fq], w_in[:, o_ff:o_ff + FX_HEADS]], axis=1)
    w_main = jnp.concatenate(
        [w_in[:, o_qk:o_i], w_in[:, o_fq:o_fv],
         jnp.pad(w_gate, ((0, 0), (0, LANES - _N_GT)))], axis=1).astype(BF16)
    w_t = jnp.concatenate([w_in[:, o_fv:o_ff], w_gate], axis=1).T.astype(BF16)
    bias = jnp.concatenate([b_i, b_f, fx_b_f]).astype(F32)
    brow = jnp.pad(bias, (0, LANES - _N_GT)).reshape(1, LANES)
    bcol = bias.reshape(_N_GT, 1)

    mq, mk, v, o, gate, gatet, fq, fk, fvt = _proj(
        x, ln1.reshape(1, -1), w_main, w_t, brow, bcol, conv_w, conv_b.reshape(1, -1),
        tm=t["tm"], tkc=t["tkc"], L=t["chunk"])
    ml = _mlstm(mq, mk, v, o, gate, gatet, ml_norm.reshape(1, -1), L=t["chunk"],
                NB=max(n for n in range(1, t["ml_rows"] + 1) if B % n == 0))
    fx = _fox(fq, fk, fvt, tq=t["tq"], ts=t["ts"], tkc=t["tkc"], qb=t["qb"])
    kmem, vmem = _memkv(mem, ln_mem.reshape(1, -1), w_xkv.astype(BF16))
    x2 = _xattn(x, ml, fx, w_out.astype(BF16), ln_x.reshape(1, -1), w_xq.astype(BF16),
                kmem, vmem, w_xo.astype(BF16), tm=t["tail"])
    return _mlp(x2, ln2.reshape(1, -1), w_ff1.astype(BF16), w_ff2.astype(BF16),
                ln_f_or_none.reshape(1, -1), tm=t["tail"])


def kernel(x, mem, ln1, w_in, ml_conv_w, ml_conv_b, ml_b_i, ml_b_f, ml_norm, fx_b_f, w_out,
           ln_x, ln_mem, w_xq, w_xkv, w_xo, ln2, w_ff1, w_ff2, ln_f):
    depth = w_in.shape[0]
    assert depth == 1, "single-layer problem: the final norm is fused into the layer's MLP kernel"
    return _layer(x, mem, ln1[0], w_in[0], ml_conv_w[0], ml_conv_b[0], ml_b_i[0], ml_b_f[0],
                  ml_norm[0], fx_b_f[0], w_out[0], ln_x[0], ln_mem[0], w_xq[0], w_xkv[0],
                  w_xo[0], ln2[0], w_ff1[0], w_ff2[0], ln_f)
```

```python
import functools

import jax
import jax.numpy as jnp
from jax import lax
from jax.experimental import pallas as pl
from jax.experimental.pallas import tpu as pltpu

F32 = jnp.float32
BF16 = jnp.bfloat16

D_MODEL = 1024
ML_HEADS = 4
ML_DH = 128
ML_W = ML_HEADS * ML_DH
CONV_W = 4
FX_HEADS = 8
FX_DH = 64
FX_W = FX_HEADS * FX_DH
X_HEADS = 4
X_DH = D_MODEL // X_HEADS
D_FF = 4 * D_MODEL
EPS = 1e-6

LANES = 128
NEG_BIG = -1e30
LOG2E = 1.4426950408889634
_TAIL = 8
VMEM_LIMIT = 56 * 1024 * 1024

_C_QK = 0
_C_V = _C_QK + 2 * ML_W
_C_O = _C_V + ML_W
_C_FQ = _C_O + ML_W
_C_FK = _C_FQ + FX_W
_C_G = _C_FK + FX_W
_C_END = _C_G + LANES
_R_GT = FX_W
_N_GT = 16

_AUG0 = FX_DH


def _tiles(S):
    tm = min(512, S)
    return dict(
        tm=tm,
        chunk=min(256, S),
        ml_rows=4,
        tq=min(256, S),
        ts=min(256, S),
        qb=min(8, S // min(256, S)),
        tkc=max(tm, min(2048, S)),
        tail=min(1024, S),
    )


def _rms(x, g):
    return x * lax.rsqrt(jnp.mean(x * x, axis=-1, keepdims=True) + EPS) * g


def _split3(x):
    hi = x.astype(BF16)
    r = x - hi.astype(F32)
    mid = r.astype(BF16)
    lo = (r - mid.astype(F32)).astype(BF16)
    return hi, mid, lo


def _log_sigmoid(x):
    return jnp.minimum(x, 0.0) - jnp.log1p(jnp.exp(-jnp.abs(x)))


def _const_spec(shape):
    nd = len(shape)
    return pl.BlockSpec(shape, lambda *_: (0,) * nd, pipeline_mode=pl.Buffered(1))


def _proj_kernel(x_ref, g_ref, w_ref, wt_ref, brow_ref, bcol_ref, cw_ref, cb_ref,
                 mq_ref, mk_ref, v_ref, o_ref, gate_ref, gatet_ref, fq_ref, fk_ref, fvt_ref,
                 carry_ref, zbuf_ref, *, tm, L):
    @pl.when(pl.program_id(1) == 0)
    def _():
        carry_ref[...] = jnp.zeros_like(carry_ref)
        zbuf_ref[0:_TAIL, :] = jnp.zeros((_TAIL, 2 * ML_W), F32)

    x = x_ref[0]
    h = _rms(x, g_ref[...]).astype(BF16)

    def proj(c0, c1):
        return jnp.dot(h, w_ref[:, c0:c1], preferred_element_type=F32)

    lane = lax.broadcasted_iota(jnp.int32, (tm, LANES), 1)

    def in_lanes(lo, n):
        return (lane >= lo) & (lane < lo + n)

    gpre = proj(_C_G, _C_END) + brow_ref[...]
    zq = proj(_C_FQ, _C_FK) * (FX_DH ** -0.5 * LOG2E)
    zk = proj(_C_FK, _C_G)
    gls = _log_sigmoid(gpre)
    tril = (lax.broadcasted_iota(jnp.int32, (L, L), 1)
            <= lax.broadcasted_iota(jnp.int32, (L, L), 0)).astype(BF16)
    is_ig = lax.broadcasted_iota(jnp.int32, (L, LANES), 1) < ML_HEADS
    parts = _split3(gls)
    carry = carry_ref[...]
    csum_chunks = []
    for ci in range(tm // L):
        rows = slice(ci * L, (ci + 1) * L)
        local = sum(jnp.dot(tril, part[rows], preferred_element_type=F32) for part in parts)
        gate_ref[0, rows, :] = LOG2E * jnp.where(is_ig, gpre[rows], local)
        csum_chunks.append(carry + local)
        carry = carry + local[L - 1:L, :]
    carry_ref[...] = carry
    csum = jnp.concatenate(csum_chunks, axis=0)

    c_hi, c_mid, c_lo = (p.astype(F32) for p in _split3(csum * LOG2E))
    src = 2 * ML_HEADS
    cq = jnp.where(in_lanes(_AUG0, 8), pltpu.roll(c_hi, _AUG0 - src, 1),
         jnp.where(in_lanes(_AUG0 + 8, 8), pltpu.roll(c_mid, _AUG0 + 8 - src, 1),
         jnp.where(in_lanes(_AUG0 + 16, 8), pltpu.roll(c_lo, _AUG0 + 16 - src, 1), 0.0)))
    ck = -pltpu.roll(cq, 24, 1)
    lane_row = lax.broadcasted_iota(jnp.int32, (1, LANES), 1)
    for hd in range(FX_HEADS):
        grp = hd // 2
        zq_h = zq[:, grp * LANES:(grp + 1) * LANES]
        zk_h = zk[:, grp * LANES:(grp + 1) * LANES]
        if hd % 2:
            zq_h = pltpu.roll(zq_h, LANES // 2, 1)
            zk_h = pltpu.roll(zk_h, LANES // 2, 1)
        hot = [(lane_row == base0 + 8 * j + hd) for j in range(3) for base0 in (_AUG0, _AUG0 + 24)]
        hot_k = (hot[0] | hot[2] | hot[4]).astype(F32)
        hot_q = (hot[1] | hot[3] | hot[5]).astype(F32)
        fq_ref[0, hd] = jnp.where(lane < _AUG0, zq_h, cq + hot_q).astype(BF16)
        fk_ref[0, hd] = jnp.where(lane < _AUG0, zk_h, ck + hot_k).astype(BF16)

    zqk = proj(_C_QK, _C_V)
    zbuf_ref[_TAIL:_TAIL + tm, :] = zqk
    conv = cb_ref[...] + zqk * cw_ref[CONV_W - 1:CONV_W, :]
    for j in range(CONV_W - 1):
        off = _TAIL - (CONV_W - 1) + j
        conv = conv + zbuf_ref[off:off + tm, :] * cw_ref[j:j + 1, :]
    zbuf_ref[0:_TAIL, :] = zbuf_ref[tm:tm + _TAIL, :]
    act = conv * jax.nn.sigmoid(conv)
    mq_ref[0] = act[:, 0:ML_W].astype(BF16)
    mk_ref[0] = (act[:, ML_W:2 * ML_W] * (ML_DH ** -0.5)).astype(BF16)

    zt = lax.dot_general(wt_ref[...], h, (((1,), (1,)), ((), ())), preferred_element_type=F32)
    fvt_ref[0, 0] = zt[0:FX_W, :].astype(BF16)
    gt = zt[_R_GT:_R_GT + _N_GT, :] + bcol_ref[...]
    glt = _log_sigmoid(gt)
    row = lax.broadcasted_iota(jnp.int32, (_N_GT, L), 0)
    triu = (lax.broadcasted_iota(jnp.int32, (L, L), 0)
            <= lax.broadcasted_iota(jnp.int32, (L, L), 1)).astype(BF16)
    for ci in range(tm // L):
        cols = slice(ci * L, (ci + 1) * L)
        brow = sum(jnp.dot(part[:, cols], triu, preferred_element_type=F32)
                   for part in _split3(glt))
        gatet_ref[0, :, cols] = LOG2E * jnp.where(row < ML_HEADS, gt[:, cols], brow)

    v_ref[0] = proj(_C_V, _C_O).astype(BF16)
    o_ref[0] = proj(_C_O, _C_FQ).astype(BF16)


def _proj(x, ln1, w_main, w_t, brow, bcol, conv_w, conv_b, *, tm, tkc, L):
    B, S, _ = x.shape
    ns = S // tm
    r = tkc // tm
    kern = functools.partial(_proj_kernel, tm=tm, L=L)
    row_spec = lambda n: pl.BlockSpec((1, tm, n), lambda b, s: (b, s, 0))
    out_shape = (
        jax.ShapeDtypeStruct((B, S, ML_W), BF16),
        jax.ShapeDtypeStruct((B, S, ML_W), BF16),
        jax.ShapeDtypeStruct((B, S, ML_W), BF16),
        jax.ShapeDtypeStruct((B, S, ML_W), BF16),
        jax.ShapeDtypeStruct((B, S, LANES), F32),
        jax.ShapeDtypeStruct((B, _N_GT, S), F32),
        jax.ShapeDtypeStruct((B, FX_HEADS, S, LANES), BF16),
        jax.ShapeDtypeStruct((B, FX_HEADS, S, LANES), BF16),
        jax.ShapeDtypeStruct((B, S // tkc, FX_W, tkc), BF16),
    )
    out_specs = (
        row_spec(ML_W), row_spec(ML_W), row_spec(ML_W), row_spec(ML_W), row_spec(LANES),
        pl.BlockSpec((1, _N_GT, tm), lambda b, s: (b, 0, s)),
        pl.BlockSpec((1, FX_HEADS, tm, LANES), lambda b, s: (b, 0, s, 0)),
        pl.BlockSpec((1, FX_HEADS, tm, LANES), lambda b, s: (b, 0, s, 0)),
        pl.BlockSpec((1, 1, FX_W, tm), lambda b, s: (b, s // r, 0, s % r)),
    )
    return pl.pallas_call(
        kern,
        grid=(B, ns),
        in_specs=[
            row_spec(D_MODEL),
            _const_spec((1, D_MODEL)),
            _const_spec(w_main.shape),
            _const_spec(w_t.shape),
            _const_spec((1, LANES)),
            _const_spec((_N_GT, 1)),
            _const_spec((CONV_W, 2 * ML_W)),
            _const_spec((1, 2 * ML_W)),
        ],
        out_specs=out_specs,
        out_shape=out_shape,
        scratch_shapes=[pltpu.VMEM((1, LANES), F32),
                        pltpu.VMEM((tm + _TAIL, 2 * ML_W), F32)],
        compiler_params=pltpu.CompilerParams(
            dimension_semantics=("arbitrary", "arbitrary"), vmem_limit_bytes=VMEM_LIMIT),
        name="proj",
    )(x, ln1, w_main, w_t, brow, bcol, conv_w, conv_b)


_ML_SL = [slice(hd * ML_DH, (hd + 1) * ML_DH) for hd in range(ML_HEADS)]
_V_INTER, _V_EM, _V_WA, _V_DECAY = 0, 1, 2, 3


def _mlstm_gates(gate_ref, gatet_ref, m_ref, et_ref, vec_ref, slot, *, L, NB):
    src_i = lax.broadcasted_iota(jnp.int32, (L, L), 0)
    tgt_i = lax.broadcasted_iota(jnp.int32, (L, L), 1)
    causal_t = src_i <= tgt_i
    gates = [(gate_ref[nb], gatet_ref[nb]) for nb in range(NB)]
    for ch in range(NB * ML_HEADS):
        nb, hd = divmod(ch, ML_HEADS)
        gate, gatet = gates[nb]
        ig_row = gatet[hd:hd + 1, :]
        b_row = gatet[ML_HEADS + hd:ML_HEADS + hd + 1, :]
        r_col = gate[:, hd:hd + 1] - gate[:, ML_HEADS + hd:ML_HEADS + hd + 1]
        g = b_row[:, L - 1:L]
        m_prev = m_ref[ch:ch + 1, 0:1]
        dmat_t = jnp.where(causal_t, b_row + r_col, NEG_BIG)
        m_inter = b_row + m_prev
        m_t = jnp.maximum(m_inter, jnp.max(dmat_t, axis=0, keepdims=True))
        a_row = g - b_row + ig_row
        m_new = jnp.maximum(g + m_prev, jnp.max(a_row, axis=1, keepdims=True))
        m_ref[ch:ch + 1, :] = jnp.broadcast_to(m_new, (1, LANES))
        et_ref[slot, ch] = jnp.exp2(dmat_t - m_t)
        vec_ref[slot, ch, _V_INTER:_V_INTER + 1, :] = jnp.exp2(m_inter - m_t)
        vec_ref[slot, ch, _V_EM:_V_EM + 1, :] = jnp.exp2(-m_t)
        vec_ref[slot, ch, _V_WA:_V_WA + 1, :] = jnp.exp2(a_row - m_new)
        vec_ref[slot, ch, _V_DECAY:_V_DECAY + 1, :] = jnp.broadcast_to(
            jnp.exp2(g + m_prev - m_new), (1, L))


def _mlstm_mix(q_ref, k_ref, v_ref, et_ref, vec_ref, state_ref, hvt_ref, slot, *, L, NB):
    one_rows = (lax.broadcasted_iota(jnp.int32, (ML_DH, L), 0) == 0).astype(BF16)
    heads = range(NB * ML_HEADS)
    rows = lambda ref, ch: ref[ch // ML_HEADS, :, _ML_SL[ch % ML_HEADS]]
    vec = [vec_ref[slot, hd] for hd in heads]
    q = [rows(q_ref, hd) for hd in heads]
    k = [rows(k_ref, hd) for hd in heads]
    vaug_t = [jnp.concatenate([rows(v_ref, hd).T, one_rows], axis=0)
              for hd in heads]
    vaug_tw = [(vaug_t[hd].astype(F32) * vec[hd][_V_WA:_V_WA + 1, :]).astype(BF16) for hd in heads]
    s_t = [lax.dot_general(k[hd], q[hd], (((1,), (1,)), ((), ())),
                           preferred_element_type=F32) for hd in heads]
    qs_t = [lax.dot_general(state_ref[hd].astype(BF16), q[hd], (((1,), (1,)), ((), ())),
                            preferred_element_type=F32) for hd in heads]
    upd = [jnp.dot(vaug_tw[hd], k[hd], preferred_element_type=F32) for hd in heads]
    for hd in heads:
        state_ref[hd] = vec[hd][_V_DECAY:_V_DECAY + 1, 0:1] * state_ref[hd] + upd[hd]
    scores_t = [(s_t[hd] * et_ref[slot, hd]).astype(BF16) for hd in heads]
    nd_t = [jnp.dot(vaug_t[hd], scores_t[hd], preferred_element_type=F32)
            + vec[hd][_V_INTER:_V_INTER + 1, :] * qs_t[hd] for hd in heads]
    for hd in heads:
        den = nd_t[hd][ML_DH:ML_DH + 1, :]
        hvt_ref[slot, hd] = nd_t[hd][0:ML_DH, :] / jnp.maximum(
            jnp.abs(den), vec[hd][_V_EM:_V_EM + 1, :])


def _mlstm_out(hvt_ref, o_ref, nrm_ref, out_ref, slot, *, NB):
    for ch in range(NB * ML_HEADS):
        nb, hd = divmod(ch, ML_HEADS)
        hv_t = hvt_ref[slot, ch]
        hn_t = hv_t * lax.rsqrt(jnp.mean(hv_t * hv_t, axis=0, keepdims=True) + EPS)
        og = jax.nn.sigmoid(o_ref[nb, :, _ML_SL[hd]].astype(F32))
        out_ref[nb, :, _ML_SL[hd]] = (hn_t.T * nrm_ref[:, _ML_SL[hd]] * og).astype(BF16)


def _mlstm_kernel(q_ref, k_ref, v_ref, o_ref, gate0_ref, gatet0_ref, gate_ref, gatet_ref, nrm_ref,
                  out_ref, state_ref, m_ref, e_ref, vec_ref, hv_ref, *, L, NB):
    c = pl.program_id(1)

    @pl.when(c == 0)
    def _():
        state_ref[...] = jnp.zeros_like(state_ref)
        m_ref[...] = jnp.zeros_like(m_ref)
        hv_ref[...] = jnp.zeros_like(hv_ref)
        _mlstm_gates(gate0_ref, gatet0_ref, m_ref, e_ref, vec_ref, 0, L=L, NB=NB)

    slot = c % 2
    _mlstm_out(hv_ref, o_ref, nrm_ref, out_ref, 1 - slot, NB=NB)
    _mlstm_mix(q_ref, k_ref, v_ref, e_ref, vec_ref, state_ref, hv_ref, slot, L=L, NB=NB)
    _mlstm_gates(gate_ref, gatet_ref, m_ref, e_ref, vec_ref, 1 - slot, L=L, NB=NB)


def _mlstm(q, k, v, o, gate, gatet, nrm, *, L, NB):
    B, S, _ = q.shape
    nc = S // L
    nch = NB * ML_HEADS
    kern = functools.partial(_mlstm_kernel, L=L, NB=NB)
    cur = lambda b, c: (b, jnp.minimum(c, nc - 1), 0)
    prev = lambda b, c: (b, jnp.maximum(c - 1, 0), 0)
    row_spec = lambda n, im: pl.BlockSpec((NB, L, n), im)
    return pl.pallas_call(
        kern,
        grid=(B // NB, nc + 1),
        in_specs=[
            row_spec(ML_W, cur), row_spec(ML_W, cur), row_spec(ML_W, cur), row_spec(ML_W, prev),
            row_spec(LANES, lambda b, c: (b, 0, 0)),
            pl.BlockSpec((NB, _N_GT, L), lambda b, c: (b, 0, 0)),
            row_spec(LANES, lambda b, c: (b, jnp.minimum(c + 1, nc - 1), 0)),
            pl.BlockSpec((NB, _N_GT, L), lambda b, c: (b, 0, jnp.minimum(c + 1, nc - 1))),
            _const_spec((1, ML_W)),
        ],
        out_specs=row_spec(ML_W, prev),
        out_shape=jax.ShapeDtypeStruct((B, S, ML_W), BF16),
        scratch_shapes=[
            pltpu.VMEM((nch, 2 * ML_DH, ML_DH), F32),
            pltpu.VMEM((-(-nch // 8) * 8, LANES), F32),
            pltpu.VMEM((2, nch, L, L), F32),
            pltpu.VMEM((2, nch, 4, L), F32),
            pltpu.VMEM((2, nch, ML_DH, L), F32),
        ],
        compiler_params=pltpu.CompilerParams(
            dimension_semantics=("arbitrary", "arbitrary"), vmem_limit_bytes=VMEM_LIMIT),
        name="mlstm",
    )(q, k, v, o, gate, gatet, gate, gatet, nrm)


_HP = 4
_VROWS = FX_DH + 16
_LAG = 3 * _HP


def _fox_kernel(q_ref, k_ref, vt_ref, o_ref, m_ref, acc_ref, p_ref, *, tq, ts, tkc, qb):
    nsub = tkc // ts
    per = ts // tq
    key_i = lax.broadcasted_iota(jnp.int32, (ts, tq), 0)
    qry_i = lax.broadcasted_iota(jnp.int32, (ts, tq), 1)
    ones_rows = (lax.broadcasted_iota(jnp.int32, (_VROWS - FX_DH, ts), 0) == 0).astype(BF16)

    def query_block(qq, _):
        qi = pl.program_id(2) * qb + qq
        q_rows = pl.ds(pl.multiple_of(qq * tq, tq), tq)
        n_full = qi // (nsub * per)
        rem = qi % (nsub * per)

        m_ref[...] = jnp.full_like(m_ref, NEG_BIG)
        acc_ref[...] = jnp.zeros_like(acc_ref)

        def chunk(js, n_plain, diagonal, q_off=0):
            state = [(m_ref[hd], acc_ref[hd]) for hd in range(_HP)]
            tiles = [(c, i, hd) for c in range(len(js))
                     for i in range(n_plain + int(diagonal)) for hd in range(_HP)]
            scores = {}

            def qk(c, i, hd):
                st = lax.dot_general(k_ref[0, hd, js[c], i * ts:(i + 1) * ts, :],
                                     q_ref[0, hd, q_rows, :], (((1,), (1,)), ((), ())),
                                     preferred_element_type=F32)
                if diagonal and i == n_plain:
                    st = jnp.where(qry_i + q_off >= key_i, st, NEG_BIG)
                scores[c, i, hd] = st

            alphas = {}

            def softmax(t, c, i, hd):
                m, acc = state[hd]
                st = scores.pop((c, i, hd))
                m_new = jnp.maximum(m, jnp.max(st, axis=0, keepdims=True))
                alphas[c, i, hd] = jnp.exp2(m - m_new)
                p_ref[t] = jnp.exp2(st - m_new).astype(BF16)
                state[hd] = (m_new, acc)

            def pv(t, c, i, hd):
                m, acc = state[hd]
                vt = jnp.concatenate(
                    [vt_ref[0, js[c], hd * FX_DH:(hd + 1) * FX_DH, i * ts:(i + 1) * ts],
                     ones_rows], axis=0)
                acc = alphas.pop((c, i, hd)) * acc + jnp.dot(
                    vt, p_ref[t], preferred_element_type=F32)
                state[hd] = (m, acc)

            for t in range(len(tiles) + _LAG):
                if t < len(tiles):
                    qk(*tiles[t])
                if _HP <= t < len(tiles) + _HP:
                    softmax(t - _HP, *tiles[t - _HP])
                if t >= _LAG:
                    pv(t - _LAG, *tiles[t - _LAG])
            for hd in range(_HP):
                m_ref[hd], acc_ref[hd] = state[hd]

        def pair(jp, _):
            chunk([2 * jp, 2 * jp + 1], nsub, False)
            return 0

        lax.fori_loop(0, n_full // 2, pair, 0)
        pl.when(n_full % 2 == 1)(functools.partial(chunk, [n_full - 1], nsub, False))
        for r in range(nsub * per):
            pl.when(rem == r)(functools.partial(chunk, [n_full], r // per, True, (r % per) * tq))
        out_t = jnp.concatenate(
            [acc_ref[hd, 0:FX_DH, :] / acc_ref[hd, FX_DH:FX_DH + 1, :] for hd in range(_HP)],
            axis=0)
        o_ref[0, q_rows, :] = out_t.T.astype(BF16)
        return 0

    lax.fori_loop(0, qb, query_block, 0)


def _fox(fq, fk, fvt, *, tq, ts, tkc, qb):
    B, H, S, _ = fq.shape
    n = S // tkc
    fk5 = fk.reshape(B, H, n, tkc, LANES)
    kern = functools.partial(_fox_kernel, tq=tq, ts=ts, tkc=tkc, qb=qb)
    return pl.pallas_call(
        kern,
        grid=(B, H // _HP, S // (tq * qb)),
        in_specs=[
            pl.BlockSpec((1, _HP, tq * qb, LANES), lambda b, h, q: (b, h, q, 0)),
            pl.BlockSpec((1, _HP, n, tkc, LANES), lambda b, h, q: (b, h, 0, 0, 0)),
            pl.BlockSpec((1, n, _HP * FX_DH, tkc), lambda b, h, q: (b, 0, h, 0)),
        ],
        out_specs=pl.BlockSpec((1, tq * qb, _HP * FX_DH), lambda b, h, q: (b, q, h)),
        out_shape=jax.ShapeDtypeStruct((B, S, FX_W), BF16),
        scratch_shapes=[
            pltpu.VMEM((_HP, 1, tq), F32),
            pltpu.VMEM((_HP, _VROWS, tq), F32),
            pltpu.VMEM((2 * (tkc // ts) * _HP, ts, tq), BF16),
        ],
        compiler_params=pltpu.CompilerParams(
            dimension_semantics=("arbitrary", "arbitrary", "arbitrary"),
            vmem_limit_bytes=VMEM_LIMIT),
        name="fox",
    )(fq, fk5, fvt)


def _memkv_kernel(mem_ref, g_ref, w_ref, k_ref, v_ref):
    mn = _rms(mem_ref[0], g_ref[...]).astype(BF16)
    k_ref[0] = jnp.dot(mn, w_ref[:, 0:D_MODEL], preferred_element_type=F32).astype(BF16)
    v_ref[0] = jnp.dot(mn, w_ref[:, D_MODEL:2 * D_MODEL], preferred_element_type=F32).astype(BF16)


def _memkv(mem, ln_mem, w_xkv):
    B, M, _ = mem.shape
    spec = pl.BlockSpec((1, M, D_MODEL), lambda b: (b, 0, 0))
    return pl.pallas_call(
        _memkv_kernel,
        grid=(B,),
        in_specs=[spec, _const_spec((1, D_MODEL)), _const_spec(w_xkv.shape)],
        out_specs=(spec, spec),
        out_shape=(jax.ShapeDtypeStruct((B, M, D_MODEL), BF16),) * 2,
        compiler_params=pltpu.CompilerParams(
            dimension_semantics=("arbitrary",), vmem_limit_bytes=VMEM_LIMIT),
        name="memkv",
    )(mem, ln_mem, w_xkv)


_ROW_GROUPS = 2


def _xattn_kernel(x_ref, ml_ref, fx_ref, wout_ref, lnx_ref, wq_ref, km_ref, vm_ref, wo_ref,
                  out_ref):
    tm = x_ref.shape[1]
    groups = [slice(r, r + tm // _ROW_GROUPS) for r in range(0, tm, tm // _ROW_GROUPS)]
    sls = [slice(hd * X_DH, (hd + 1) * X_DH) for hd in range(X_HEADS)]
    km = km_ref[0]
    vm = vm_ref[0]
    x1 = [x_ref[0, g, :]
          + jnp.dot(ml_ref[0, g, :], wout_ref[0:ML_W, :], preferred_element_type=F32)
          + jnp.dot(fx_ref[0, g, :], wout_ref[ML_W:ML_W + FX_W, :], preferred_element_type=F32)
          for g in groups]
    q = [(jnp.dot(_rms(x, lnx_ref[...]).astype(BF16), wq_ref[...], preferred_element_type=F32)
          * (X_DH ** -0.5)).astype(BF16) for x in x1]
    s = [[lax.dot_general(qg[:, sl], km[:, sl], (((1,), (1,)), ((), ())),
                          preferred_element_type=F32) for sl in sls] for qg in q]
    o = []
    for sg in s:
        heads = []
        for hd, sl in enumerate(sls):
            p = jnp.exp(sg[hd] - jnp.max(sg[hd], axis=-1, keepdims=True))
            l = jnp.sum(p, axis=-1, keepdims=True)
            oh = jnp.dot(p.astype(BF16), vm[:, sl], preferred_element_type=F32) / l
            heads.append(oh.astype(BF16))
        o.append(jnp.concatenate(heads, axis=1))
    for g, xg, og in zip(groups, x1, o):
        out_ref[0, g, :] = xg + jnp.dot(og, wo_ref[...], preferred_element_type=F32)


def _xattn(x, ml, fx, w_out, ln_x, w_xq, kmem, vmem, w_xo, *, tm):
    B, S, _ = x.shape
    M = kmem.shape[1]
    row_spec = lambda n: pl.BlockSpec((1, tm, n), lambda b, s: (b, s, 0))
    mem_spec = pl.BlockSpec((1, M, D_MODEL), lambda b, s: (b, 0, 0))
    return pl.pallas_call(
        _xattn_kernel,
        grid=(B, S // tm),
        in_specs=[
            row_spec(D_MODEL), row_spec(ML_W), row_spec(FX_W),
            _const_spec(w_out.shape), _const_spec((1, D_MODEL)), _const_spec(w_xq.shape),
            mem_spec, mem_spec, _const_spec(w_xo.shape),
        ],
        out_specs=row_spec(D_MODEL),
        out_shape=jax.ShapeDtypeStruct((B, S, D_MODEL), F32),
        compiler_params=pltpu.CompilerParams(
            dimension_semantics=("arbitrary", "arbitrary"), vmem_limit_bytes=VMEM_LIMIT),
        name="xattn",
    )(x, ml, fx, w_out, ln_x, w_xq, kmem, vmem, w_xo)


_FF_CHUNK = 1024


def _mlp_kernel(x_ref, ln2_ref, w1_ref, w2_ref, lnf_ref, out_ref):
    x = x_ref[0]
    h = _rms(x, ln2_ref[...]).astype(BF16)
    acc = x
    for c in range(D_FF // _FF_CHUNK):
        sl = slice(c * _FF_CHUNK, (c + 1) * _FF_CHUNK)
        u = jnp.maximum(jnp.dot(h, w1_ref[:, sl], preferred_element_type=F32), 0.0)
        acc = acc + jnp.dot((u * u).astype(BF16), w2_ref[sl, :], preferred_element_type=F32)
    out_ref[0] = _rms(acc, lnf_ref[...])


def _mlp(x, ln2, w1, w2, ln_f, *, tm):
    B, S, _ = x.shape
    row_spec = pl.BlockSpec((1, tm, D_MODEL), lambda b, s: (b, s, 0))
    return pl.pallas_call(
        _mlp_kernel,
        grid=(B, S // tm),
        in_specs=[row_spec, _const_spec((1, D_MODEL)), _const_spec(w1.shape),
                  _const_spec(w2.shape), _const_spec((1, D_MODEL))],
        out_specs=row_spec,
        out_shape=jax.ShapeDtypeStruct((B, S, D_MODEL), F32),
        compiler_params=pltpu.CompilerParams(
            dimension_semantics=("arbitrary", "arbitrary"), vmem_limit_bytes=VMEM_LIMIT),
        name="mlp",
    )(x, ln2, w1, w2, ln_f)


def _layer(x, mem, ln1, w_in, conv_w, conv_b, b_i, b_f, ml_norm, fx_b_f, w_out,
           ln_x, ln_mem, w_xq, w_xkv, w_xo, ln2, w_ff1, w_ff2, ln_f_or_none):
    B, S, _ = x.shape
    t = _tiles(S)
    o_qk, o_v, o_o = 0, 2 * ML_W, 3 * ML_W
    o_i = 4 * ML_W
    o_f = o_i + ML_HEADS
    o_fq = o_f + ML_HEADS
    o_fk, o_fv = o_fq + FX_W, o_fq + 2 * FX_W
    o_ff = o_fq + 3 * FX_W
    w_gate = jnp.concatenate(
        [w_in[:, o_i:o_fq], w_in[:, o_ff:o_ff + FX_HEADS]], axis=1)
    w_main = jnp.concatenate(
        [w_in[:, o_qk:o_i], w_in[:, o_fq:o_fv],
         jnp.pad(w_gate, ((0, 0), (0, LANES - _N_GT)))], axis=1).astype(BF16)
    w_t = jnp.concatenate([w_in[:, o_fv:o_ff], w_gate], axis=1).T.astype(BF16)
    bias = jnp.concatenate([b_i, b_f, fx_b_f]).astype(F32)
    brow = jnp.pad(bias, (0, LANES - _N_GT)).reshape(1, LANES)
    bcol = bias.reshape(_N_GT, 1)

    mq, mk, v, o, gate, gatet, fq, fk, fvt = _proj(
        x, ln1.reshape(1, -1), w_main, w_t, brow, bcol, conv_w, conv_b.reshape(1, -1),
        tm=t["tm"], tkc=t["tkc"], L=t["chunk"])
    ml = _mlstm(mq, mk, v, o, gate, gatet, ml_norm.reshape(1, -1), L=t["chunk"],
                NB=max(n for n in range(1, t["ml_rows"] + 1) if B % n == 0))
    fx = _fox(fq, fk, fvt, tq=t["tq"], ts=t["ts"], tkc=t["tkc"], qb=t["qb"])
    kmem, vmem = _memkv(mem, ln_mem.reshape(1, -1), w_xkv.astype(BF16))
    x2 = _xattn(x, ml, fx, w_out.astype(BF16), ln_x.reshape(1, -1), w_xq.astype(BF16),
                kmem, vmem, w_xo.astype(BF16), tm=t["tail"])
    return _mlp(x2, ln2.reshape(1, -1), w_ff1.astype(BF16), w_ff2.astype(BF16),
                ln_f_or_none.reshape(1, -1), tm=t["tail"])


def kernel(x, mem, ln1, w_in, ml_conv_w, ml_conv_b, ml_b_i, ml_b_f, ml_norm, fx_b_f, w_out,
           ln_x, ln_mem, w_xq, w_xkv, w_xo, ln2, w_ff1, w_ff2, ln_f):
    depth = w_in.shape[0]
    assert depth == 1, "single-layer problem: the final norm is fused into the layer's MLP kernel"
    return _layer(x, mem, ln1[0], w_in[0], ml_conv_w[0], ml_conv_b[0], ml_b_i[0], ml_b_f[0],
                  ml_norm[0], fx_b_f[0], w_out[0], ln_x[0], ln_mem[0], w_xq[0], w_xkv[0],
                  w_xo[0], ln2[0], w_ff1[0], w_ff2[0], ln_f)
```
